```python
import math
import jax
import jax.numpy as jnp
from jax import lax
import numpy as np

D_MODEL = 2048
BATCH = 4
SEQ = 2048
DEPTH = 4
DEC_BATCH = 32
DEC_SEQ = 8
PAST_LEN = 16384
PAGE_SIZE = 128

N_MIXERS = 3
BLOCK = 128
ROPE_THETA = 10000.0
NORM_EPS = 1e-5
A_PATTERNS = ((128, 1), (512, 4), (2048, 16))
A_GROUPS = len(A_PATTERNS)
A_HEAD_DIM = 128
A_HEADS = D_MODEL // 256
B_HEAD_DIM = 64
B_HEADS = D_MODEL // B_HEAD_DIM
B_KV_HEADS = B_HEADS // 8
B_WINDOW = 128
CONV_W = 31
D_FF = 11 * D_MODEL // 4
N_EXPERTS = 8
TOP_K = 2
D_FF_EXPERT = 7 * D_MODEL // 2
MOE_BLOCK = 128
N_A_LAYERS = len(range(0, DEPTH, N_MIXERS))
N_B_LAYERS = len(range(1, DEPTH, N_MIXERS))
N_C_LAYERS = len(range(2, DEPTH, N_MIXERS))
N_DENSE_LAYERS = len(range(0, DEPTH, 2))
N_MOE_LAYERS = len(range(1, DEPTH, 2))

kernel_name = 'hybrid_dilated_swa_conformer_moe_step'


def rmsnorm(x, g):
    xf = x.astype(jnp.float32)
    y = xf * lax.rsqrt(jnp.mean(xf * xf, axis=-1, keepdims=True) + NORM_EPS)
    return (y * g.astype(jnp.float32)).astype(x.dtype)


def layernorm(x, g, b):
    xf = x.astype(jnp.float32)
    mu = jnp.mean(xf, axis=-1, keepdims=True)
    xc = xf - mu
    var = jnp.mean(xc * xc, axis=-1, keepdims=True)
    y = xc * lax.rsqrt(var + NORM_EPS) * g.astype(jnp.float32) + b.astype(jnp.float32)
    return y.astype(x.dtype)


def rope(x, pos):
    hd = x.shape[-1]
    half = hd // 2
    inv = jnp.exp(jnp.arange(half, dtype=jnp.float32) * (-2.0 * math.log(ROPE_THETA) / hd))
    ang = pos.astype(jnp.float32)[:, None] * inv[None, :]
    bshape = (pos.shape[0],) + (1,) * (x.ndim - 3) + (half,)
    cos = jnp.cos(ang).reshape(bshape)
    sin = jnp.sin(ang).reshape(bshape)
    xf = x.astype(jnp.float32)
    x1, x2 = xf[..., :half], xf[..., half:]
    return jnp.concatenate([x1 * cos - x2 * sin, x2 * cos + x1 * sin], axis=-1).astype(x.dtype)


def masked_softmax(s, mask, sink):
    s = jnp.where(mask, s, -jnp.inf)
    lse = jax.nn.logsumexp(s, axis=-1)
    if sink is not None:
        lse = jnp.logaddexp(lse, sink.astype(jnp.float32))
    return jnp.exp(s - lse[..., None]), lse


def banded_window_attn(q, k, v, window, sinks=None):
    n, L, H, hd = q.shape
    G = k.shape[2]
    R = H // G
    nb = -(-L // BLOCK)
    Lp = nb * BLOCK
    pad = ((0, 0), (0, Lp - L), (0, 0), (0, 0))
    q, k, v = jnp.pad(q, pad), jnp.pad(k, pad), jnp.pad(v, pad)
    qb = q.reshape(n, nb, BLOCK, G, R, hd)

    def with_prev(t):
        tb = t.reshape(n, nb, BLOCK, G, hd)
        prev = jnp.pad(tb, ((0, 0), (1, 0), (0, 0), (0, 0), (0, 0)))[:, :nb]
        return jnp.concatenate([prev, tb], axis=2)

    kw, vw = with_prev(k), with_prev(v)
    s = jnp.einsum('nbqgrd,nbkgd->nbgrqk', qb, kw, preferred_element_type=jnp.float32) * (hd ** -0.5)
    iq = jnp.arange(BLOCK)[:, None]
    ik = jnp.arange(2 * BLOCK)[None, :]
    dist = BLOCK + iq - ik
    kpos = (jnp.arange(nb)[:, None, None] - 1) * BLOCK + ik[None]
    mask = (dist >= 0)[None] & (dist <= window)[None] & (kpos >= 0)
    sink = None if sinks is None else sinks.reshape(G, R)[:, :, None]
    p, lse = masked_softmax(s, mask[None, :, None, None], sink)
    o = jnp.einsum('nbgrqk,nbkgd->nbqgrd', p.astype(vw.dtype), vw).reshape(n, Lp, H, hd)[:, :L]
    lse = lse.transpose(0, 1, 4, 2, 3).reshape(n, Lp, H)[:, :L]
    return o, lse


def dilated_prompt_attn(q, k, v, window, dilation):
    B, L, H, hd = q.shape
    m = L // dilation

    def to_sub(t):
        return t.reshape(B, m, dilation, H, hd).transpose(0, 2, 1, 3, 4).reshape(B * dilation, m, H, hd)

    o, lse = banded_window_attn(to_sub(q), to_sub(k), to_sub(v), window // dilation)
    o = o.reshape(B, dilation, m, H, hd).transpose(0, 2, 1, 3, 4).reshape(B, L, H, hd)
    lse = lse.reshape(B, dilation, m, H).transpose(0, 2, 1, 3).reshape(B, L, H)
    return o, lse


def strided_window_decode(q, k_cat, v_cat, n_past, window, dilation, sinks=None):
    n, T, H, hd = q.shape
    G = k_cat.shape[2]
    R = H // G
    n_keys = window // dilation + 1
    idx = n_past + jnp.arange(T)[:, None] - dilation * jnp.arange(n_keys)[None, :]
    valid = idx >= 0
    idx = jnp.maximum(idx, 0)
    kg = k_cat[:, idx]
    vg = v_cat[:, idx]
    s = jnp.einsum('ntgrd,ntkgd->ntgrk', q.reshape(n, T, G, R, hd), kg,
                   preferred_element_type=jnp.float32) * (hd ** -0.5)
    sink = None if sinks is None else sinks.reshape(G, R)
    p, lse = masked_softmax(s, valid[None, :, None, None, :], sink)
    o = jnp.einsum('ntgrk,ntkgd->ntgrd', p.astype(vg.dtype), vg).reshape(n, T, H, hd)
    return o, lse.reshape(n, T, H)


def mixer_a(h, pos, w_qkv, w_o, bufs):
    B, L, _ = h.shape
    qkv = (h @ w_qkv).reshape(B, L, 3, A_GROUPS, A_HEADS, A_HEAD_DIM)
    q = rope(qkv[:, :, 0], pos)
    k = rope(qkv[:, :, 1], pos)
    v = qkv[:, :, 2]
    outs, lses, states = [], [], []
    for g, (window, dil) in enumerate(A_PATTERNS):
        qg, kg, vg = q[:, :, g], k[:, :, g], v[:, :, g]
        kv_new = jnp.stack([kg, vg], axis=2)
        if bufs is None:
            o, lse = dilated_prompt_attn(qg, kg, vg, window, dil)
            states.append(kv_new[:, L - min(window, L):])
        else:
            buf = bufs[g]
            kv_cat = jnp.concatenate([buf, kv_new], axis=1)
            o, lse = strided_window_decode(qg, kv_cat[:, :, 0], kv_cat[:, :, 1], buf.shape[1], window, dil)
            states.append(kv_new)
        outs.append(o)
        lses.append(lse)
    wts = jax.nn.softmax(jnp.stack(lses), axis=0)
    o = jnp.einsum('gblh,gblhd->blhd', wts, jnp.stack(outs).astype(jnp.float32)).astype(h.dtype)
    return o.reshape(B, L, A_HEADS * A_HEAD_DIM) @ w_o, tuple(states)


def mixer_b(h, pos, w_qkv, b_qkv, sinks, w_o, b_o, buf):
    B, L, _ = h.shape
    nq = B_HEADS * B_HEAD_DIM
    nk = B_KV_HEADS * B_HEAD_DIM
    qkv = h @ w_qkv + b_qkv
    q = rope(qkv[..., :nq].reshape(B, L, B_HEADS, B_HEAD_DIM), pos)
    k = rope(qkv[..., nq:nq + nk].reshape(B, L, B_KV_HEADS, B_HEAD_DIM), pos)
    v = qkv[..., nq + nk:].reshape(B, L, B_KV_HEADS, B_HEAD_DIM)
    kv_new = jnp.stack([k, v], axis=2)
    if buf is None:
        o, _ = banded_window_attn(q, k, v, B_WINDOW, sinks)
        state = kv_new[:, L - min(B_WINDOW, L):]
    else:
        kv_cat = jnp.concatenate([buf, kv_new], axis=1)
        o, _ = strided_window_decode(q, kv_cat[:, :, 0], kv_cat[:, :, 1], buf.shape[1], B_WINDOW, 1, sinks)
        state = kv_new
    return o.reshape(B, L, nq) @ w_o + b_o, state


def mixer_c(h, w_in, b_in, w_dw, b_dw, ln_g, ln_b, w_out, b_out, buf):
    B, L, D = h.shape
    a, gate = jnp.split(h @ w_in + b_in, 2, axis=-1)
    u = a * jax.nn.sigmoid(gate)
    if buf is None:
        u_ext = jnp.pad(u, ((0, 0), (CONV_W - 1, 0), (0, 0)))
        state = u[:, L - (CONV_W - 1):]
    else:
        u_ext = jnp.concatenate([buf, u], axis=1)
        state = u_ext[:, -(CONV_W - 1):]
    c = lax.conv_general_dilated(u_ext, w_dw[:, None, :], window_strides=(1,), padding='VALID',
                                 dimension_numbers=('NWC', 'WIO', 'NWC'), feature_group_count=D)
    c = jax.nn.silu(layernorm(c + b_dw, ln_g, ln_b))
    return c @ w_out + b_out, state


def swiglu(x, w_gate, w_up, w_down):
    return (jax.nn.silu(x @ w_gate) * (x @ w_up)) @ w_down


def moe_swiglu(x, w_router, w_gate, w_up, w_down):
    shape = x.shape
    xf = x.reshape(-1, shape[-1])
    n_tok = xf.shape[0]
    n_asg = n_tok * TOP_K
    logits = (xf @ w_router).astype(jnp.float32)
    top_logit, top_e = lax.top_k(logits, TOP_K)
    gates = jax.nn.softmax(top_logit, axis=-1)
    flat_e = top_e.reshape(-1)
    flat_t = jnp.repeat(jnp.arange(n_tok, dtype=jnp.int32), TOP_K)
    order = jnp.argsort(flat_e)
    se, st, sg = flat_e[order], flat_t[order], gates.reshape(-1)[order]
    counts = jnp.bincount(flat_e, length=N_EXPERTS)
    start = jnp.cumsum(counts) - counts
    padded = (counts + MOE_BLOCK - 1) // MOE_BLOCK * MOE_BLOCK
    pend = jnp.cumsum(padded)
    dest = (pend - padded)[se] + jnp.arange(n_asg) - start[se]
    n_rows = -(-(n_asg + N_EXPERTS * (MOE_BLOCK - 1)) // MOE_BLOCK) * MOE_BLOCK
    n_blk = n_rows // MOE_BLOCK
    row_tok = jnp.full((n_rows,), n_tok, jnp.int32).at[dest].set(st)
    row_gate = jnp.zeros((n_rows,), jnp.float32).at[dest].set(sg)
    blk_e = jnp.minimum(jnp.searchsorted(pend, jnp.arange(n_blk) * MOE_BLOCK, side='right'), N_EXPERTS - 1)
    x_ext = jnp.concatenate([xf, jnp.zeros((1, xf.shape[1]), xf.dtype)], axis=0)

    def expert_block(args):
        tok, e = args
        hb = x_ext[tok]
        return (jax.nn.silu(hb @ w_gate[e]) * (hb @ w_up[e])) @ w_down[e]

    yb = lax.map(expert_block, (row_tok.reshape(n_blk, MOE_BLOCK), blk_e))
    yr = (yb.reshape(n_rows, -1) * row_gate[:, None]).astype(x.dtype)
    out = jnp.zeros_like(x_ext).at[row_tok].add(yr)[:n_tok]
    return out.reshape(shape)


def forward(x, pos, p, past):
    new_a, new_b, new_c = [], [], []
    for i in range(DEPTH):
        t, j = i % N_MIXERS, i // N_MIXERS
        h = rmsnorm(x, p['norm_mix'][i])
        if t == 0:
            bufs = None if past is None else tuple(c[j] for c in past['a'])
            y, st = mixer_a(h, pos, p['a_w_qkv'][j], p['a_w_o'][j], bufs)
            new_a.append(st)
        elif t == 1:
            buf = None if past is None else past['b'][j]
            y, st = mixer_b(h, pos, p['b_w_qkv'][j], p['b_b_qkv'][j], p['b_sinks'][j],
                            p['b_w_o'][j], p['b_b_o'][j], buf)
            new_b.append(st)
        else:
            buf = None if past is None else past['c'][j]
            y, st = mixer_c(h, p['c_w_in'][j], p['c_b_in'][j], p['c_w_dw'][j], p['c_b_dw'][j],
                            p['c_ln_g'][j], p['c_ln_b'][j], p['c_w_out'][j], p['c_b_out'][j], buf)
            new_c.append(st)
        x = x + y
        h = rmsnorm(x, p['norm_ffn'][i])
        f = i // 2
        if i % 2 == 0:
            x = x + swiglu(h, p['f_w_gate'][f], p['f_w_up'][f], p['f_w_down'][f])
        else:
            x = x + moe_swiglu(h, p['m_w_router'][f], p['m_w_gate'][f], p['m_w_up'][f], p['m_w_down'][f])
    y = rmsnorm(x, p['norm_final'])
    states = (jnp.stack([s[0] for s in new_a]), jnp.stack([s[1] for s in new_a]),
              jnp.stack([s[2] for s in new_a]), jnp.stack(new_b), jnp.stack(new_c))
    return y, states


def _normal(k, shape, scale):
    return jax.random.normal(k, shape, jnp.float32) * scale


def setup_inputs(seed: int = 0) -> dict:
    key = jax.random.key(seed)
    keys = iter(jax.random.split(key, 40))
    D = D_MODEL
    qkv_a = 3 * A_GROUPS * A_HEADS * A_HEAD_DIM
    qkv_b = (B_HEADS + 2 * B_KV_HEADS) * B_HEAD_DIM
    return {
        'x_prompt': _normal(next(keys), (BATCH, SEQ, D), 1.0),
        'x_sample': _normal(next(keys), (DEC_BATCH, DEC_SEQ, D), 1.0),
        'cache_a_g0': _normal(next(keys), (N_A_LAYERS, DEC_BATCH, min(A_PATTERNS[0][0], PAST_LEN), 2, A_HEADS, A_HEAD_DIM), 1.0),
        'cache_a_g1': _normal(next(keys), (N_A_LAYERS, DEC_BATCH, min(A_PATTERNS[1][0], PAST_LEN), 2, A_HEADS, A_HEAD_DIM), 1.0),
        'cache_a_g2': _normal(next(keys), (N_A_LAYERS, DEC_BATCH, min(A_PATTERNS[2][0], PAST_LEN), 2, A_HEADS, A_HEAD_DIM), 1.0),
        'cache_b_kv': _normal(next(keys), (N_B_LAYERS, DEC_BATCH, min(B_WINDOW, PAST_LEN), 2, B_KV_HEADS, B_HEAD_DIM), 1.0),
        'state_c_conv': _normal(next(keys), (N_C_LAYERS, DEC_BATCH, CONV_W - 1, D), 0.5),
        'norm_mix': 1.0 + _normal(next(keys), (DEPTH, D), 0.02),
        'norm_ffn': 1.0 + _normal(next(keys), (DEPTH, D), 0.02),
        'norm_final': 1.0 + _normal(next(keys), (D,), 0.02),
        'a_w_qkv': _normal(next(keys), (N_A_LAYERS, D, qkv_a), D ** -0.5),
        'a_w_o': _normal(next(keys), (N_A_LAYERS, A_HEADS * A_HEAD_DIM, D), (A_HEADS * A_HEAD_DIM) ** -0.5),
        'b_w_qkv': _normal(next(keys), (N_B_LAYERS, D, qkv_b), D ** -0.5),
        'b_b_qkv': _normal(next(keys), (N_B_LAYERS, qkv_b), 0.02),
        'b_sinks': _normal(next(keys), (N_B_LAYERS, B_HEADS), 0.5),
        'b_w_o': _normal(next(keys), (N_B_LAYERS, B_HEADS * B_HEAD_DIM, D), (B_HEADS * B_HEAD_DIM) ** -0.5),
        'b_b_o': _normal(next(keys), (N_B_LAYERS, D), 0.02),
        'c_w_in': _normal(next(keys), (N_C_LAYERS, D, 2 * D), D ** -0.5),
        'c_b_in': _normal(next(keys), (N_C_LAYERS, 2 * D), 0.02),
        'c_w_dw': _normal(next(keys), (N_C_LAYERS, CONV_W, D), CONV_W ** -0.5),
        'c_b_dw': _normal(next(keys), (N_C_LAYERS, D), 0.02),
        'c_ln_g': 1.0 + _normal(next(keys), (N_C_LAYERS, D), 0.02),
        'c_ln_b': _normal(next(keys), (N_C_LAYERS, D), 0.02),
        'c_w_out': _normal(next(keys), (N_C_LAYERS, D, D), D ** -0.5),
        'c_b_out': _normal(next(keys), (N_C_LAYERS, D), 0.02),
        'f_w_gate': _normal(next(keys), (N_DENSE_LAYERS, D, D_FF), D ** -0.5),
        'f_w_up': _normal(next(keys), (N_DENSE_LAYERS, D, D_FF), D ** -0.5),
        'f_w_down': _normal(next(keys), (N_DENSE_LAYERS, D_FF, D), D_FF ** -0.5),
        'm_w_router': _normal(next(keys), (N_MOE_LAYERS, D, N_EXPERTS), D ** -0.5),
        'm_w_gate': _normal(next(keys), (N_MOE_LAYERS, N_EXPERTS, D, D_FF_EXPERT), D ** -0.5),
        'm_w_up': _normal(next(keys), (N_MOE_LAYERS, N_EXPERTS, D, D_FF_EXPERT), D ** -0.5),
        'm_w_down': _normal(next(keys), (N_MOE_LAYERS, N_EXPERTS, D_FF_EXPERT, D), D_FF_EXPERT ** -0.5),
    }


def reference(x_prompt, x_sample, cache_a_g0, cache_a_g1, cache_a_g2, cache_b_kv, state_c_conv,
              norm_mix, norm_ffn, norm_final, a_w_qkv, a_w_o, b_w_qkv, b_b_qkv, b_sinks, b_w_o, b_b_o,
              c_w_in, c_b_in, c_w_dw, c_b_dw, c_ln_g, c_ln_b, c_w_out, c_b_out,
              f_w_gate, f_w_up, f_w_down, m_w_router, m_w_gate, m_w_up, m_w_down):
    params = dict(norm_mix=norm_mix, norm_ffn=norm_ffn, norm_final=norm_final,
                  a_w_qkv=a_w_qkv, a_w_o=a_w_o,
                  b_w_qkv=b_w_qkv, b_b_qkv=b_b_qkv, b_sinks=b_sinks, b_w_o=b_w_o, b_b_o=b_b_o,
                  c_w_in=c_w_in, c_b_in=c_b_in, c_w_dw=c_w_dw, c_b_dw=c_b_dw,
                  c_ln_g=c_ln_g, c_ln_b=c_ln_b, c_w_out=c_w_out, c_b_out=c_b_out,
                  f_w_gate=f_w_gate, f_w_up=f_w_up, f_w_down=f_w_down,
                  m_w_router=m_w_router, m_w_gate=m_w_gate, m_w_up=m_w_up, m_w_down=m_w_down)
    past = dict(a=(cache_a_g0, cache_a_g1, cache_a_g2), b=cache_b_kv, c=state_c_conv)
    pos_prompt = jnp.arange(x_prompt.shape[1], dtype=jnp.int32)
    pos_sample = PAST_LEN + jnp.arange(x_sample.shape[1], dtype=jnp.int32)
    y_prompt, (pa0, pa1, pa2, pb, pc) = forward(x_prompt, pos_prompt, params, None)
    y_sample, (sa0, sa1, sa2, sb, sc) = forward(x_sample, pos_sample, params, past)
    return (y_prompt, y_sample, pa0, pa1, pa2, pb, pc, sa0, sa1, sa2, sb, sc)
```

```python
import functools
import math

import jax
import jax.numpy as jnp
import numpy as np
from jax import lax
from jax.experimental import pallas as pl
from jax.experimental.pallas import tpu as pltpu

D_MODEL = 2048
BATCH = 4
SEQ = 2048
DEPTH = 4
DEC_BATCH = 32
DEC_SEQ = 8
PAST_LEN = 16384
N_MIXERS = 3
BLOCK = 128
ROPE_THETA = 10000.0
NORM_EPS = 1e-5
A_PATTERNS = ((128, 1), (512, 4), (2048, 16))
A_GROUPS = len(A_PATTERNS)
A_HEAD_DIM = 128
A_HEADS = D_MODEL // 256
B_HEAD_DIM = 64
B_HEADS = D_MODEL // B_HEAD_DIM
B_KV_HEADS = B_HEADS // 8
B_WINDOW = 128
CONV_W = 31
N_EXPERTS = 8
TOP_K = 2

N_PROMPT = BATCH * SEQ
N_SAMPLE = DEC_BATCH * DEC_SEQ
M_TOK = N_PROMPT + N_SAMPLE

LANES = 128
VMEM_LIMIT_BYTES = 56 * 1024 * 1024

TM = 1056
FFN_TM = 704
TR = 528
TF = 256
MOE_TILE = 256
MOE_SB_TILES = 4
MOE_SB = MOE_TILE * MOE_SB_TILES
CONV_TL = 256
CONV_HALO = 32
NEG = -1e30

BF16 = jnp.bfloat16
F32 = jnp.float32


def _params(n_axes):
    return pltpu.CompilerParams(dimension_semantics=("arbitrary",) * n_axes,
                                vmem_limit_bytes=VMEM_LIMIT_BYTES)


def _nt_dot(a, b):
    return lax.dot_general(a, b, (((1,), (1,)), ((), ())), preferred_element_type=F32)


def _rms(x, g):
    return x * lax.rsqrt(jnp.mean(x * x, axis=-1, keepdims=True) + NORM_EPS) * g


def _rmsnorm_kernel(x_ref, g_ref, h_ref):
    h_ref[...] = _rms(x_ref[...], g_ref[...]).astype(h_ref.dtype)


def rmsnorm(x, g, out_dtype, tr=TR):
    m, d = x.shape
    return pl.pallas_call(
        _rmsnorm_kernel,
        grid=(m // tr,),
        in_specs=[pl.BlockSpec((tr, d), lambda i: (i, 0)),
                  pl.BlockSpec((1, d), lambda i: (0, 0))],
        out_specs=pl.BlockSpec((tr, d), lambda i: (i, 0)),
        out_shape=jax.ShapeDtypeStruct((m, d), out_dtype),
        compiler_params=_params(1),
        name="rmsnorm",
    )(x, g.reshape(1, d))


def _add_norm_kernel(x_ref, y_ref, g_ref, xo_ref, h_ref):
    x = x_ref[...] + y_ref[...]
    xo_ref[...] = x
    h_ref[...] = _rms(x, g_ref[...]).astype(h_ref.dtype)


def add_norm(x, y, g, out_dtype, tr=TR):
    m, d = x.shape
    row = pl.BlockSpec((tr, d), lambda i: (i, 0))
    return pl.pallas_call(
        _add_norm_kernel,
        grid=(m // tr,),
        in_specs=[row, row, pl.BlockSpec((1, d), lambda i: (0, 0))],
        out_specs=[row, row],
        out_shape=[jax.ShapeDtypeStruct((m, d), F32), jax.ShapeDtypeStruct((m, d), out_dtype)],
        compiler_params=_params(1),
        name="add_norm",
    )(x, y, g.reshape(1, d))


def _add_norm_router_kernel(x_ref, y_ref, g_ref, wr_ref, xo_ref, h_ref, lg_ref):
    x = x_ref[...] + y_ref[...]
    xo_ref[...] = x
    h = _rms(x, g_ref[...])
    h_ref[...] = h.astype(h_ref.dtype)
    lg_ref[...] = jnp.dot(h, wr_ref[...], preferred_element_type=F32,
                          precision=lax.Precision.HIGHEST)


def add_norm_router(x, y, g, w_router, tr=TR):
    m, d = x.shape
    ne = w_router.shape[1]
    wr = jnp.pad(w_router, ((0, 0), (0, LANES - ne)))
    row = pl.BlockSpec((tr, d), lambda i: (i, 0))
    xo, h, lg = pl.pallas_call(
        _add_norm_router_kernel,
        grid=(m // tr,),
        in_specs=[row, row, pl.BlockSpec((1, d), lambda i: (0, 0)),
                  pl.BlockSpec((d, LANES), lambda i: (0, 0))],
        out_specs=[row, row, pl.BlockSpec((tr, LANES), lambda i: (i, 0))],
        out_shape=[jax.ShapeDtypeStruct((m, d), F32), jax.ShapeDtypeStruct((m, d), BF16),
                   jax.ShapeDtypeStruct((m, LANES), F32)],
        compiler_params=_params(1),
        name="add_norm_router",
    )(x, y, g.reshape(1, d), wr)
    return xo, h, lg[:, :ne]


def _token_positions():
    pos_p = np.tile(np.arange(SEQ, dtype=np.int32), BATCH)
    pos_s = np.tile(PAST_LEN + np.arange(DEC_SEQ, dtype=np.int32), DEC_BATCH)
    return jnp.asarray(np.concatenate([pos_p, pos_s]))


def rope_tables(head_dim):
    half = head_dim // 2
    pos = _token_positions()
    inv = jnp.exp(jnp.arange(half, dtype=F32) * (-2.0 * math.log(ROPE_THETA) / head_dim))
    ang = pos.astype(F32)[:, None] * inv[None, :]
    cos, sin = jnp.cos(ang), jnp.sin(ang)
    reps = LANES // head_dim
    zero = jnp.zeros_like(sin)
    cos_t = jnp.tile(jnp.concatenate([cos, cos], axis=1), (1, reps))
    sin_lo = jnp.tile(jnp.concatenate([-sin, zero], axis=1), (1, reps))
    sin_hi = jnp.tile(jnp.concatenate([zero, sin], axis=1), (1, reps))
    return cos_t, sin_lo, sin_hi


def _rope_chunk(a, cos, sin_lo, sin_hi, half):
    if 2 * half == LANES:
        return a * cos + pltpu.roll(a, half, 1) * (sin_lo + sin_hi)
    return a * cos + pltpu.roll(a, LANES - half, 1) * sin_lo + pltpu.roll(a, half, 1) * sin_hi


def _mm_kernel(*refs, has_bias, rope_cols, rope_half, tn):
    it = iter(refs)
    x_ref, w_ref = next(it), next(it)
    b_ref = next(it) if has_bias else None
    if rope_cols:
        cos_ref, slo_ref, shi_ref = next(it), next(it), next(it)
    o_ref, wb_ref = next(it), next(it)
    j, i = pl.program_id(0), pl.program_id(1)

    @pl.when(i == 0)
    def _():
        wb_ref[...] = w_ref[...].astype(BF16)

    acc = jnp.dot(x_ref[...], wb_ref[...], preferred_element_type=F32)
    if has_bias:
        acc = acc + b_ref[...]
    if not rope_cols:
        o_ref[...] = acc.astype(o_ref.dtype)
        return

    n_chunks = tn // LANES

    def store(n_rope_chunks):
        cos, slo, shi = cos_ref[...], slo_ref[...], shi_ref[...]
        for c in range(n_chunks):
            a = acc[:, c * LANES:(c + 1) * LANES]
            if c < n_rope_chunks:
                a = _rope_chunk(a, cos, slo, shi, rope_half)
            o_ref[:, c * LANES:(c + 1) * LANES] = a.astype(o_ref.dtype)

    full_blocks, rem = divmod(rope_cols, tn)

    @pl.when(j < full_blocks)
    def _():
        store(n_chunks)

    if rem:
        @pl.when(j == full_blocks)
        def _():
            store(rem // LANES)

    @pl.when(j >= full_blocks + (1 if rem else 0))
    def _():
        store(0)


def matmul(x, w, layer, bias=None, rope=None, rope_cols=0, rope_half=0, tm=TM, tn=1024,
           out_dtype=F32):
    m, k = x.shape
    n = w.shape[2]
    assert m % tm == 0 and n % tn == 0 and rope_cols % LANES == 0
    in_specs = [pl.BlockSpec((tm, k), lambda j, i: (i, 0)),
                pl.BlockSpec((None, k, tn), lambda j, i: (layer, 0, j))]
    args = [x, w]
    if bias is not None:
        in_specs.append(pl.BlockSpec((1, tn), lambda j, i: (0, j)))
        args.append(bias.reshape(1, n))
    if rope_cols:
        for t in rope:
            in_specs.append(pl.BlockSpec((tm, LANES), lambda j, i: (i, 0)))
            args.append(t)
    kern = functools.partial(_mm_kernel, has_bias=bias is not None, rope_cols=rope_cols,
                             rope_half=rope_half, tn=tn)
    return pl.pallas_call(
        kern,
        grid=(n // tn, m // tm),
        in_specs=in_specs,
        out_specs=pl.BlockSpec((tm, tn), lambda j, i: (i, j)),
        out_shape=jax.ShapeDtypeStruct((m, n), out_dtype),
        scratch_shapes=[pltpu.VMEM((k, tn), BF16)],
        compiler_params=_params(2),
        name="matmul",
    )(*args)


def _mm_glu_kernel(x_ref, wa_ref, wg_ref, ba_ref, bg_ref, o_ref, wab_ref, wgb_ref):
    @pl.when(pl.program_id(1) == 0)
    def _():
        wab_ref[...] = wa_ref[...].astype(BF16)
        wgb_ref[...] = wg_ref[...].astype(BF16)

    x = x_ref[...]
    a = jnp.dot(x, wab_ref[...], preferred_element_type=F32) + ba_ref[...]
    g = jnp.dot(x, wgb_ref[...], preferred_element_type=F32) + bg_ref[...]
    o_ref[...] = a * jax.nn.sigmoid(g)


def matmul_glu(x, w, layer, bias, tm=TM, tn=512):
    m, k = x.shape
    n = w.shape[2] // 2
    nb = n // tn
    b2 = bias.reshape(1, 2 * n)
    return pl.pallas_call(
        _mm_glu_kernel,
        grid=(nb, m // tm),
        in_specs=[pl.BlockSpec((tm, k), lambda j, i: (i, 0)),
                  pl.BlockSpec((None, k, tn), lambda j, i: (layer, 0, j)),
                  pl.BlockSpec((None, k, tn), lambda j, i: (layer, 0, j + nb)),
                  pl.BlockSpec((1, tn), lambda j, i: (0, j)),
                  pl.BlockSpec((1, tn), lambda j, i: (0, j + nb))],
        out_specs=pl.BlockSpec((tm, tn), lambda j, i: (i, j)),
        out_shape=jax.ShapeDtypeStruct((m, n), F32),
        scratch_shapes=[pltpu.VMEM((k, tn), BF16), pltpu.VMEM((k, tn), BF16)],
        compiler_params=_params(2),
        name="matmul_glu",
    )(x, w, w, b2, b2)


def _swiglu_tile(x, wg, wu, wd):
    g = jnp.dot(x, wg, preferred_element_type=F32)
    u = jnp.dot(x, wu, preferred_element_type=F32)
    h = (g * jax.nn.sigmoid(g) * u).astype(BF16)
    return jnp.dot(h, wd, preferred_element_type=F32)


def _ffn_kernel(x_ref, wg_ref, wu_ref, wd_ref, o_ref):
    f = pl.program_id(1)
    y = _swiglu_tile(x_ref[...], wg_ref[...].astype(BF16), wu_ref[...].astype(BF16),
                     wd_ref[...].astype(BF16))

    @pl.when(f == 0)
    def _():
        o_ref[...] = y

    @pl.when(f > 0)
    def _():
        o_ref[...] += y


def swiglu_ffn(h, w_gate, w_up, w_down, layer, tm=FFN_TM, tf=TF):
    m, d = h.shape
    ff = w_gate.shape[2]
    assert m % tm == 0 and ff % tf == 0
    return pl.pallas_call(
        _ffn_kernel,
        grid=(m // tm, ff // tf),
        in_specs=[pl.BlockSpec((tm, d), lambda i, f: (i, 0)),
                  pl.BlockSpec((None, d, tf), lambda i, f: (layer, 0, f)),
                  pl.BlockSpec((None, d, tf), lambda i, f: (layer, 0, f)),
                  pl.BlockSpec((None, tf, d), lambda i, f: (layer, f, 0))],
        out_specs=pl.BlockSpec((tm, d), lambda i, f: (i, 0)),
        out_shape=jax.ShapeDtypeStruct((m, d), F32),
        compiler_params=_params(2),
        name="swiglu_ffn",
    )(h, w_gate, w_up, w_down)


def _moe_kernel(sb_e_ref, sb_nt_ref, x_ref, wg_ref, wu_ref, wd_ref, o_ref,
                wgb_ref, wub_ref, wdb_ref):
    s, f = pl.program_id(0), pl.program_id(1)
    nt = sb_nt_ref[s]

    @pl.when(nt > 0)
    def _():
        wgb_ref[...] = wg_ref[...].astype(BF16)
        wub_ref[...] = wu_ref[...].astype(BF16)
        wdb_ref[...] = wd_ref[...].astype(BF16)

    for r in range(MOE_SB_TILES):
        rows = slice(r * MOE_TILE, (r + 1) * MOE_TILE)

        @pl.when(r < nt)
        def _():
            y = _swiglu_tile(x_ref[rows, :], wgb_ref[...], wub_ref[...], wdb_ref[...])

            @pl.when(f == 0)
            def _():
                o_ref[rows, :] = y

            @pl.when(f > 0)
            def _():
                o_ref[rows, :] += y

        @pl.when(jnp.logical_and(r >= nt, f == 0))
        def _():
            o_ref[rows, :] = jnp.zeros((MOE_TILE, o_ref.shape[1]), F32)


def moe_ffn(x_sorted, sb_expert, sb_ntiles, w_gate, w_up, w_down, layer, tf=TF):
    n_rows, d = x_sorted.shape
    ff = w_gate.shape[3]
    n_sb = n_rows // MOE_SB
    n_f = ff // tf

    def f_idx(s, f, nt_ref):
        return jnp.where(nt_ref[s] > 0, f, n_f - 1)

    grid_spec = pltpu.PrefetchScalarGridSpec(
        num_scalar_prefetch=2,
        grid=(n_sb, n_f),
        in_specs=[pl.BlockSpec((MOE_SB, d), lambda s, f, e, nt: (s, 0)),
                  pl.BlockSpec((None, None, d, tf),
                               lambda s, f, e, nt: (layer, e[s], 0, f_idx(s, f, nt))),
                  pl.BlockSpec((None, None, d, tf),
                               lambda s, f, e, nt: (layer, e[s], 0, f_idx(s, f, nt))),
                  pl.BlockSpec((None, None, tf, d),
                               lambda s, f, e, nt: (layer, e[s], f_idx(s, f, nt), 0))],
        out_specs=pl.BlockSpec((MOE_SB, d), lambda s, f, e, nt: (s, 0)),
        scratch_shapes=[pltpu.VMEM((d, tf), BF16), pltpu.VMEM((d, tf), BF16),
                        pltpu.VMEM((tf, d), BF16)],
    )
    return pl.pallas_call(
        _moe_kernel,
        grid_spec=grid_spec,
        out_shape=jax.ShapeDtypeStruct((n_rows, d), F32),
        compiler_params=_params(2),
        name="moe_ffn",
    )(sb_expert, sb_ntiles, x_sorted, w_gate, w_up, w_down)


def moe_layer(h, logits, w_gate, w_up, w_down, layer):
    t, d = h.shape
    n_asg = t * TOP_K
    top_logit, top_e = lax.top_k(logits, TOP_K)
    gates = jax.nn.softmax(top_logit, axis=-1)
    flat_e = top_e.reshape(-1).astype(jnp.int32)
    order = jnp.argsort(flat_e)
    se = flat_e[order]
    counts = jnp.bincount(flat_e, length=N_EXPERTS).astype(jnp.int32)
    start = jnp.cumsum(counts) - counts
    n_sb_e = (counts + MOE_SB - 1) // MOE_SB
    sb_end = jnp.cumsum(n_sb_e)
    sb_start = sb_end - n_sb_e
    pos = jnp.arange(n_asg, dtype=jnp.int32) - start[se]
    dest_sorted = sb_start[se] * MOE_SB + pos
    dest = jnp.zeros((n_asg,), jnp.int32).at[order].set(dest_sorted)
    max_sb = (n_asg + N_EXPERTS * (MOE_SB - 1)) // MOE_SB
    n_rows = max_sb * MOE_SB
    row_tok = jnp.zeros((n_rows,), jnp.int32).at[dest_sorted].set(order // TOP_K)
    total_sb = sb_end[-1]
    sb_ids = jnp.arange(max_sb, dtype=jnp.int32)
    sb_e = jnp.searchsorted(sb_end, jnp.minimum(sb_ids, total_sb - 1), side='right').astype(jnp.int32)
    sb_e = jnp.minimum(sb_e, N_EXPERTS - 1)
    live_rows = counts[sb_e] - (sb_ids - sb_start[sb_e]) * MOE_SB
    sb_nt = jnp.clip((live_rows + MOE_TILE - 1) // MOE_TILE, 0, MOE_SB_TILES)
    sb_nt = jnp.where(sb_ids < total_sb, sb_nt, 0).astype(jnp.int32)

    x_sorted = jnp.take(h, row_tok, axis=0)
    y_sorted = moe_ffn(x_sorted, sb_e, sb_nt, w_gate, w_up, w_down, layer)
    picked = jnp.take(y_sorted, dest, axis=0).reshape(t, TOP_K, d)
    return jnp.sum(picked * gates[:, :, None], axis=1)


def _band_masks(window, first_block):
    iq = lax.broadcasted_iota(jnp.int32, (BLOCK, BLOCK), 0)
    ik = lax.broadcasted_iota(jnp.int32, (BLOCK, BLOCK), 1)
    d_cur = iq - ik
    d_prev = d_cur + BLOCK
    m_cur = jnp.logical_and(d_cur >= 0, d_cur <= window)
    m_prev = jnp.logical_and(d_prev <= window, jnp.logical_not(first_block))
    return m_cur, m_prev


def _band_head(q, kp, kc, vp, vc, m_cur, m_prev, scale, sink):
    q = q.astype(BF16)
    s_c = jnp.where(m_cur, _nt_dot(q, kc.astype(BF16)) * scale, NEG)
    s_p = jnp.where(m_prev, _nt_dot(q, kp.astype(BF16)) * scale, NEG)
    m = jnp.maximum(jnp.max(s_c, axis=1, keepdims=True), jnp.max(s_p, axis=1, keepdims=True))
    if sink is not None:
        m = jnp.maximum(m, sink)
    p_c = jnp.exp(s_c - m)
    p_p = jnp.exp(s_p - m)
    l = jnp.sum(p_c, axis=1, keepdims=True) + jnp.sum(p_p, axis=1, keepdims=True)
    if sink is not None:
        l = l + jnp.exp(sink - m)
    o = (jnp.dot(p_c.astype(BF16), vc.astype(BF16), preferred_element_type=F32)
         + jnp.dot(p_p.astype(BF16), vp.astype(BF16), preferred_element_type=F32))
    return o / l, m + jnp.log(l)


def _attn_a_prompt_kernel(q_ref, kp_ref, kc_ref, vp_ref, vc_ref, o_ref, lse_ref, *, window):
    m_cur, m_prev = _band_masks(window, pl.program_id(2) == 0)
    scale = A_HEAD_DIM ** -0.5
    for h in range(A_HEADS):
        hs = slice(h * A_HEAD_DIM, (h + 1) * A_HEAD_DIM)
        o, lse = _band_head(q_ref[:, hs], kp_ref[:, hs], kc_ref[:, hs], vp_ref[:, hs],
                            vc_ref[:, hs], m_cur, m_prev, scale, None)
        o_ref[:, hs] = o
        lse_ref[:, hs] = jnp.broadcast_to(lse, (BLOCK, A_HEAD_DIM))


def attn_a_prompt(qkv, g):
    window, dil = A_PATTERNS[g]
    hw = A_HEADS * A_HEAD_DIM
    width = qkv.shape[1]
    cb = width // hw
    sub_len = SEQ // dil
    nb = sub_len // BLOCK
    view = qkv.reshape(M_TOK // dil, dil * width)
    blk = (BLOCK, hw)

    def cur(which):
        return pl.BlockSpec(blk, lambda b, r, j: (b * nb + j, r * cb + which * A_GROUPS + g))

    def prev(which):
        return pl.BlockSpec(blk, lambda b, r, j: (b * nb + jnp.maximum(j - 1, 0),
                                                  r * cb + which * A_GROUPS + g))

    out_spec = pl.BlockSpec(blk, lambda b, r, j: (b * nb + j, r))
    out_sds = jax.ShapeDtypeStruct((N_PROMPT // dil, dil * hw), F32)
    o, lse = pl.pallas_call(
        functools.partial(_attn_a_prompt_kernel, window=window // dil),
        grid=(BATCH, dil, nb),
        in_specs=[cur(0), prev(1), cur(1), prev(2), cur(2)],
        out_specs=[out_spec, out_spec],
        out_shape=[out_sds, out_sds],
        compiler_params=_params(3),
        name="attn_a_prompt",
    )(view, view, view, view, view)
    return o.reshape(N_PROMPT, hw), lse.reshape(N_PROMPT, hw)


def _attn_b_prompt_kernel(q_ref, kp_ref, kc_ref, vp_ref, vc_ref, sink_ref, o_ref):
    m_cur, m_prev = _band_masks(B_WINDOW, pl.program_id(1) == 0)
    scale = B_HEAD_DIM ** -0.5
    rep = B_HEADS // B_KV_HEADS
    for g in range(B_KV_HEADS):
        gs = slice(g * B_HEAD_DIM, (g + 1) * B_HEAD_DIM)
        kp, kc, vp, vc = kp_ref[:, gs], kc_ref[:, gs], vp_ref[:, gs], vc_ref[:, gs]
        for r in range(rep):
            h = g * rep + r
            hs = slice(h * B_HEAD_DIM, (h + 1) * B_HEAD_DIM)
            o, _ = _band_head(q_ref[:, hs], kp, kc, vp, vc, m_cur, m_prev, scale,
                              sink_ref[:, h:h + 1])
            o_ref[:, hs] = o.astype(o_ref.dtype)


def attn_b_prompt(qkv, sinks):
    nq = B_HEADS * B_HEAD_DIM
    nk = B_KV_HEADS * B_HEAD_DIM
    nb = SEQ // BLOCK
    kcol, vcol = nq // nk, nq // nk + 1

    def cur(col, w):
        return pl.BlockSpec((BLOCK, w), lambda b, j: (b * nb + j, col))

    def prev(col, w):
        return pl.BlockSpec((BLOCK, w), lambda b, j: (b * nb + jnp.maximum(j - 1, 0), col))

    sink_row = jnp.pad(sinks.reshape(1, B_HEADS), ((0, 0), (0, LANES - B_HEADS)))
    return pl.pallas_call(
        _attn_b_prompt_kernel,
        grid=(BATCH, nb),
        in_specs=[cur(0, nq), prev(kcol, nk), cur(kcol, nk), prev(vcol, nk), cur(vcol, nk),
                  pl.BlockSpec((1, LANES), lambda b, j: (0, 0))],
        out_specs=pl.BlockSpec((BLOCK, nq), lambda b, j: (b * nb + j, 0)),
        out_shape=jax.ShapeDtypeStruct((N_PROMPT, nq), BF16),
        compiler_params=_params(2),
        name="attn_b_prompt",
    )(qkv, qkv, qkv, qkv, qkv, sink_row)


def _decode_masks(rows, n_past, window, dil):
    t = lax.broadcasted_iota(jnp.int32, (rows, n_past), 0) % DEC_SEQ
    c = lax.broadcasted_iota(jnp.int32, (rows, n_past), 1)
    gap = n_past + t - c
    m_past = jnp.logical_and((gap & (dil - 1)) == 0, gap <= window)
    t2 = lax.broadcasted_iota(jnp.int32, (rows, DEC_SEQ), 0) % DEC_SEQ
    c2 = lax.broadcasted_iota(jnp.int32, (rows, DEC_SEQ), 1)
    gap2 = t2 - c2
    m_new = jnp.logical_and(jnp.logical_and(gap2 >= 0, (gap2 & (dil - 1)) == 0), gap2 <= window)
    return m_past, m_new


def _decode_softmax(q, k_past, v_past, k_new, v_new, m_past, m_new, scale, sink):
    s_p = jnp.where(m_past, _nt_dot(q, k_past) * scale, NEG)
    s_n = jnp.where(m_new, _nt_dot(q, k_new) * scale, NEG)
    m = jnp.maximum(jnp.max(s_p, axis=1, keepdims=True), jnp.max(s_n, axis=1, keepdims=True))
    if sink is not None:
        m = jnp.maximum(m, sink)
    p_p = jnp.exp(s_p - m)
    p_n = jnp.exp(s_n - m)
    l = jnp.sum(p_p, axis=1, keepdims=True) + jnp.sum(p_n, axis=1, keepdims=True)
    if sink is not None:
        l = l + jnp.exp(sink - m)
    o = (jnp.dot(p_p.astype(BF16), v_past, preferred_element_type=F32)
         + jnp.dot(p_n.astype(BF16), v_new, preferred_element_type=F32))
    return o / l, m + jnp.log(l)


def _attn_a_decode_kernel(q_ref, kn_ref, vn_ref, c_ref, o_ref, lse_ref, *, n_past, window, dil):
    hw = A_HEADS * A_HEAD_DIM
    rows = A_HEADS * DEC_SEQ
    row_head = lax.broadcasted_iota(jnp.int32, (rows, hw), 0) // DEC_SEQ
    lane_head = lax.broadcasted_iota(jnp.int32, (rows, hw), 1) // A_HEAD_DIM
    own = row_head == lane_head
    q = jnp.where(own, jnp.tile(q_ref[...], (A_HEADS, 1)), 0.0).astype(BF16)
    m_past, m_new = _decode_masks(rows, n_past, window, dil)
    o, lse = _decode_softmax(q, c_ref[:, :hw].astype(BF16), c_ref[:, hw:].astype(BF16),
                             kn_ref[...].astype(BF16), vn_ref[...].astype(BF16),
                             m_past, m_new, A_HEAD_DIM ** -0.5, None)
    o = jnp.where(own, o, 0.0)
    lse = jnp.where(own, jnp.broadcast_to(lse, (rows, hw)), 0.0)
    o_acc, lse_acc = o[0:DEC_SEQ], lse[0:DEC_SEQ]
    for h in range(1, A_HEADS):
        o_acc = o_acc + o[h * DEC_SEQ:(h + 1) * DEC_SEQ]
        lse_acc = lse_acc + lse[h * DEC_SEQ:(h + 1) * DEC_SEQ]
    o_ref[...] = o_acc
    lse_ref[...] = lse_acc


def attn_a_decode(qkv, cache, layer, g):
    window, dil = A_PATTERNS[g]
    assert dil & (dil - 1) == 0
    hw = A_HEADS * A_HEAD_DIM
    n_past = cache.shape[2]
    cview = cache.reshape(cache.shape[0], DEC_BATCH, n_past, 2 * hw)
    row0 = N_PROMPT // DEC_SEQ

    def tok(which):
        return pl.BlockSpec((DEC_SEQ, hw), lambda n: (row0 + n, which * A_GROUPS + g))

    out_spec = pl.BlockSpec((DEC_SEQ, hw), lambda n: (n, 0))
    out_sds = jax.ShapeDtypeStruct((N_SAMPLE, hw), F32)
    return pl.pallas_call(
        functools.partial(_attn_a_decode_kernel, n_past=n_past, window=window, dil=dil),
        grid=(DEC_BATCH,),
        in_specs=[tok(0), tok(1), tok(2),
                  pl.BlockSpec((None, None, n_past, 2 * hw), lambda n: (layer, n, 0, 0))],
        out_specs=[out_spec, out_spec],
        out_shape=[out_sds, out_sds],
        compiler_params=_params(1),
        name="attn_a_decode",
    )(qkv, qkv, qkv, cview)


def _merge_groups_kernel(o0, o1, o2, l0, l1, l2, out_ref):
    a, b, c = l0[...], l1[...], l2[...]
    m = jnp.maximum(jnp.maximum(a, b), c)
    wa, wb, wc = jnp.exp(a - m), jnp.exp(b - m), jnp.exp(c - m)
    num = wa * o0[...] + wb * o1[...] + wc * o2[...]
    out_ref[...] = (num / (wa + wb + wc)).astype(out_ref.dtype)


def merge_groups(outs, lses, tr):
    m, w = outs[0].shape
    row = pl.BlockSpec((tr, w), lambda i: (i, 0))
    return pl.pallas_call(
        _merge_groups_kernel,
        grid=(m // tr,),
        in_specs=[row] * 6,
        out_specs=row,
        out_shape=jax.ShapeDtypeStruct((m, w), BF16),
        compiler_params=_params(1),
        name="merge_groups",
    )(*outs, *lses)


def _attn_b_decode_kernel(q_ref, kn_ref, vn_ref, c_ref, sink_ref, o_ref, *, n_past):
    nk = B_KV_HEADS * B_HEAD_DIM
    rep = B_HEADS // B_KV_HEADS
    rows = rep * DEC_SEQ
    m_past, m_new = _decode_masks(rows, n_past, B_WINDOW, 1)
    pieces = []
    for g in range(B_KV_HEADS):
        gs = slice(g * B_HEAD_DIM, (g + 1) * B_HEAD_DIM)
        q = jnp.concatenate(
            [q_ref[:, (g * rep + r) * B_HEAD_DIM:(g * rep + r + 1) * B_HEAD_DIM] for r in range(rep)],
            axis=0).astype(BF16)
        o, _ = _decode_softmax(q, c_ref[:, gs].astype(BF16),
                               c_ref[:, nk + g * B_HEAD_DIM:nk + (g + 1) * B_HEAD_DIM].astype(BF16),
                               kn_ref[:, gs].astype(BF16), vn_ref[:, gs].astype(BF16),
                               m_past, m_new, B_HEAD_DIM ** -0.5, sink_ref[:, g:g + 1])
        pieces += [o[r * DEC_SEQ:(r + 1) * DEC_SEQ] for r in range(rep)]
    o_ref[...] = jnp.concatenate(pieces, axis=1)


def attn_b_decode(qkv, cache, layer, sinks):
    nq = B_HEADS * B_HEAD_DIM
    nk = B_KV_HEADS * B_HEAD_DIM
    rep = B_HEADS // B_KV_HEADS
    n_past = cache.shape[2]
    cview = cache.reshape(cache.shape[0], DEC_BATCH, n_past, 2 * nk)
    row0 = N_PROMPT // DEC_SEQ
    kcol, vcol = nq // nk, nq // nk + 1
    sink_tab = jnp.repeat(sinks.reshape(B_KV_HEADS, rep).T, DEC_SEQ, axis=0)
    sink_tab = jnp.pad(sink_tab, ((0, 0), (0, LANES - B_KV_HEADS)))
    return pl.pallas_call(
        functools.partial(_attn_b_decode_kernel, n_past=n_past),
        grid=(DEC_BATCH,),
        in_specs=[pl.BlockSpec((DEC_SEQ, nq), lambda n: (row0 + n, 0)),
                  pl.BlockSpec((DEC_SEQ, nk), lambda n: (row0 + n, kcol)),
                  pl.BlockSpec((DEC_SEQ, nk), lambda n: (row0 + n, vcol)),
                  pl.BlockSpec((None, None, n_past, 2 * nk), lambda n: (layer, n, 0, 0)),
                  pl.BlockSpec((rep * DEC_SEQ, LANES), lambda n: (0, 0))],
        out_specs=pl.BlockSpec((DEC_SEQ, nq), lambda n: (n, 0)),
        out_shape=jax.ShapeDtypeStruct((N_SAMPLE, nq), F32),
        compiler_params=_params(1),
        name="attn_b_decode",
    )(qkv, qkv, qkv, cview, sink_tab)


def _conv_kernel(prev_ref, cur_ref, w_ref, b_ref, g_ref, beta_ref, o_ref, ext_ref, conv_ref, *,
                 tl, zero_first):
    d = cur_ref.shape[1]
    prev = prev_ref[...]
    if zero_first:
        prev = jnp.where(pl.program_id(1) == 0, 0.0, prev)
    ext_ref[0:CONV_HALO, :] = prev
    ext_ref[CONV_HALO:, :] = cur_ref[...]
    first = CONV_HALO - (CONV_W - 1)
    for c in range(d // LANES):
        cs = slice(c * LANES, (c + 1) * LANES)
        acc = jnp.zeros((tl, LANES), F32)
        for k in range(CONV_W):
            acc = acc + ext_ref[first + k:first + k + tl, cs] * w_ref[k:k + 1, cs]
        conv_ref[:, cs] = acc
    x = conv_ref[...] + b_ref[...]
    mu = jnp.mean(x, axis=-1, keepdims=True)
    xc = x - mu
    var = jnp.mean(xc * xc, axis=-1, keepdims=True)
    y = xc * lax.rsqrt(var + NORM_EPS) * g_ref[...] + beta_ref[...]
    o_ref[...] = (y * jax.nn.sigmoid(y)).astype(o_ref.dtype)


def conv_module(u, prev_src, prev_block, n_seq, seq_len, row0, w_dw, b_dw, ln_g, ln_b, tl,
                zero_first, out_dtype):
    d = u.shape[1]
    nl = seq_len // tl
    vec = pl.BlockSpec((1, d), lambda b, j: (0, 0))
    kern = functools.partial(_conv_kernel, tl=tl, zero_first=zero_first)
    return pl.pallas_call(
        kern,
        grid=(n_seq, nl),
        in_specs=[pl.BlockSpec((CONV_HALO, d), lambda b, j: (prev_block(b, j), 0)),
                  pl.BlockSpec((tl, d), lambda b, j: (row0 // tl + b * nl + j, 0)),
                  pl.BlockSpec((CONV_W, d), lambda b, j: (0, 0)), vec, vec, vec],
        out_specs=pl.BlockSpec((tl, d), lambda b, j: (b * nl + j, 0)),
        out_shape=jax.ShapeDtypeStruct((n_seq * seq_len, d), out_dtype),
        scratch_shapes=[pltpu.VMEM((CONV_HALO + tl, d), F32), pltpu.VMEM((tl, d), F32)],
        compiler_params=_params(2),
        name="conv_module",
    )(prev_src, u, w_dw, b_dw.reshape(1, d), ln_g.reshape(1, d), ln_b.reshape(1, d))


def _kv_state(k, v, n_seq, seq_len, keep, heads, hd):
    k = k.reshape(n_seq, seq_len, heads, hd)[:, seq_len - keep:]
    v = v.reshape(n_seq, seq_len, heads, hd)[:, seq_len - keep:]
    return jnp.stack([k, v], axis=2)


def mixer_a(h, w_qkv, w_o, caches, layer, rope_a):
    hw = A_HEADS * A_HEAD_DIM
    qkv = matmul(h, w_qkv, layer, rope=rope_a, rope_cols=2 * A_GROUPS * hw,
                 rope_half=A_HEAD_DIM // 2, tn=hw)
    p_outs, p_lses, s_outs, s_lses, p_states, s_states = [], [], [], [], [], []
    for g, (window, _) in enumerate(A_PATTERNS):
        o, lse = attn_a_prompt(qkv, g)
        p_outs.append(o)
        p_lses.append(lse)
        o, lse = attn_a_decode(qkv, caches[g], layer, g)
        s_outs.append(o)
        s_lses.append(lse)
        k = qkv[:, (A_GROUPS + g) * hw:(A_GROUPS + g + 1) * hw]
        v = qkv[:, (2 * A_GROUPS + g) * hw:(2 * A_GROUPS + g + 1) * hw]
        p_states.append(_kv_state(k[:N_PROMPT], v[:N_PROMPT], BATCH, SEQ, min(window, SEQ),
                                  A_HEADS, A_HEAD_DIM))
        s_states.append(_kv_state(k[N_PROMPT:], v[N_PROMPT:], DEC_BATCH, DEC_SEQ, DEC_SEQ,
                                  A_HEADS, A_HEAD_DIM))
    o = jnp.concatenate([merge_groups(p_outs, p_lses, 512),
                         merge_groups(s_outs, s_lses, N_SAMPLE)], axis=0)
    return matmul(o, w_o, layer), p_states, s_states


def mixer_b(h, w_qkv, b_qkv, sinks, w_o, b_o, cache, layer, rope_b):
    nq = B_HEADS * B_HEAD_DIM
    nk = B_KV_HEADS * B_HEAD_DIM
    qkv = matmul(h, w_qkv, layer, bias=b_qkv, rope=rope_b, rope_cols=nq + nk,
                 rope_half=B_HEAD_DIM // 2, tn=512)
    o_p = attn_b_prompt(qkv, sinks)
    o_s = attn_b_decode(qkv, cache, layer, sinks).astype(BF16)
    k, v = qkv[:, nq:nq + nk], qkv[:, nq + nk:]
    p_state = _kv_state(k[:N_PROMPT], v[:N_PROMPT], BATCH, SEQ, min(B_WINDOW, SEQ),
                        B_KV_HEADS, B_HEAD_DIM)
    s_state = _kv_state(k[N_PROMPT:], v[N_PROMPT:], DEC_BATCH, DEC_SEQ, DEC_SEQ,
                        B_KV_HEADS, B_HEAD_DIM)
    y = matmul(jnp.concatenate([o_p, o_s], axis=0), w_o, layer, bias=b_o)
    return y, p_state, s_state


def mixer_c(h, w_in, b_in, w_dw, b_dw, ln_g, ln_b, w_out, b_out, state, layer):
    keep = CONV_W - 1
    u = matmul_glu(h, w_in, layer, b_in)
    nl = SEQ // CONV_TL
    ratio = CONV_TL // CONV_HALO
    act_p = conv_module(u, u, lambda b, j: jnp.maximum((b * nl + j) * ratio - 1, 0), BATCH, SEQ, 0,
                        w_dw, b_dw, ln_g, ln_b, CONV_TL, True, BF16)
    hist = jnp.pad(state, ((0, 0), (CONV_HALO - keep, 0), (0, 0))).reshape(DEC_BATCH * CONV_HALO, -1)
    act_s = conv_module(u, hist, lambda b, j: b, DEC_BATCH, DEC_SEQ, N_PROMPT,
                        w_dw, b_dw, ln_g, ln_b, DEC_SEQ, False, F32)
    y = matmul(jnp.concatenate([act_p, act_s.astype(BF16)], axis=0), w_out, layer, bias=b_out)
    u_p = u[:N_PROMPT].reshape(BATCH, SEQ, D_MODEL)
    u_s = u[N_PROMPT:].reshape(DEC_BATCH, DEC_SEQ, D_MODEL)
    p_state = u_p[:, SEQ - keep:]
    s_state = jnp.concatenate([state, u_s], axis=1)[:, -keep:]
    return y, p_state, s_state


def kernel(x_prompt, x_sample, cache_a_g0, cache_a_g1, cache_a_g2, cache_b_kv, state_c_conv,
           norm_mix, norm_ffn, norm_final, a_w_qkv, a_w_o, b_w_qkv, b_b_qkv, b_sinks, b_w_o, b_b_o,
           c_w_in, c_b_in, c_w_dw, c_b_dw, c_ln_g, c_ln_b, c_w_out, c_b_out,
           f_w_gate, f_w_up, f_w_down, m_w_router, m_w_gate, m_w_up, m_w_down):
    assert x_prompt.shape == (BATCH, SEQ, D_MODEL) and x_sample.shape == (DEC_BATCH, DEC_SEQ, D_MODEL)
    rope_a = rope_tables(A_HEAD_DIM)
    rope_b = rope_tables(B_HEAD_DIM)
    caches_a = (cache_a_g0, cache_a_g1, cache_a_g2)
    x = jnp.concatenate([x_prompt.reshape(N_PROMPT, D_MODEL), x_sample.reshape(N_SAMPLE, D_MODEL)])
    h = rmsnorm(x, norm_mix[0], BF16)
    pa, sa, pb, sb, pc, sc = [], [], [], [], [], []
    for i in range(DEPTH):
        t, j = i % N_MIXERS, i // N_MIXERS
        if t == 0:
            y, p_st, s_st = mixer_a(h, a_w_qkv, a_w_o, caches_a, j, rope_a)
            pa.append(p_st)
            sa.append(s_st)
        elif t == 1:
            y, p_st, s_st = mixer_b(h, b_w_qkv, b_b_qkv[j], b_sinks[j], b_w_o, b_b_o[j],
                                    cache_b_kv, j, rope_b)
            pb.append(p_st)
            sb.append(s_st)
        else:
            y, p_st, s_st = mixer_c(h, c_w_in, c_b_in[j], c_w_dw[j], c_b_dw[j], c_ln_g[j],
                                    c_ln_b[j], c_w_out, c_b_out[j], state_c_conv[j], j)
            pc.append(p_st)
            sc.append(s_st)
        f = i // 2
        if i % 2 == 0:
            x, h = add_norm(x, y, norm_ffn[i], BF16)
            y = swiglu_ffn(h, f_w_gate, f_w_up, f_w_down, f)
        else:
            x, h, logits = add_norm_router(x, y, norm_ffn[i], m_w_router[f])
            y = moe_layer(h, logits, m_w_gate, m_w_up, m_w_down, f)
        if i + 1 < DEPTH:
            x, h = add_norm(x, y, norm_mix[i + 1], BF16)
        else:
            x, h = add_norm(x, y, norm_final, F32)
    y_prompt = h[:N_PROMPT].reshape(BATCH, SEQ, D_MODEL)
    y_sample = h[N_PROMPT:].reshape(DEC_BATCH, DEC_SEQ, D_MODEL)
    stack_g = lambda sts, g: jnp.stack([s[g] for s in sts])
    return (y_prompt, y_sample,
            stack_g(pa, 0), stack_g(pa, 1), stack_g(pa, 2), jnp.stack(pb), jnp.stack(pc),
            stack_g(sa, 0), stack_g(sa, 1), stack_g(sa, 2), jnp.stack(sb), jnp.stack(sc))
```

```python
import functools
import math

import jax
import jax.numpy as jnp
import numpy as np
from jax import lax
from jax.experimental import pallas as pl
from jax.experimental.pallas import tpu as pltpu

D_MODEL = 2048
BATCH = 4
SEQ = 2048
DEPTH = 4
DEC_BATCH = 32
DEC_SEQ = 8
PAST_LEN = 16384
N_MIXERS = 3
BLOCK = 128
ROPE_THETA = 10000.0
NORM_EPS = 1e-5
A_PATTERNS = ((128, 1), (512, 4), (2048, 16))
A_GROUPS = len(A_PATTERNS)
A_HEAD_DIM = 128
A_HEADS = D_MODEL // 256
B_HEAD_DIM = 64
B_HEADS = D_MODEL // B_HEAD_DIM
B_KV_HEADS = B_HEADS // 8
B_WINDOW = 128
CONV_W = 31
N_EXPERTS = 8
TOP_K = 2

N_PROMPT = BATCH * SEQ
N_SAMPLE = DEC_BATCH * DEC_SEQ
M_TOK = N_PROMPT + N_SAMPLE

LANES = 128
VMEM_LIMIT_BYTES = 56 * 1024 * 1024

TM = 1056
TM_PROMPT = 1024
FFN_TM = 1056
TR = 528
TF = 256
MOE_TILE = 256
MOE_SB_TILES = 10
MOE_SB = MOE_TILE * MOE_SB_TILES
PERM = max(d for _, d in A_PATTERNS)
CONV_TL = 256
CONV_HALO = 32
NEG = -1e30

BF16 = jnp.bfloat16
F32 = jnp.float32


def _params(n_axes):
    return pltpu.CompilerParams(dimension_semantics=("arbitrary",) * n_axes,
                                vmem_limit_bytes=VMEM_LIMIT_BYTES)


def _nt_dot(a, b):
    return lax.dot_general(a, b, (((1,), (1,)), ((), ())), preferred_element_type=F32)


def _rms(x, g):
    return x * lax.rsqrt(jnp.mean(x * x, axis=-1, keepdims=True) + NORM_EPS) * g


def _rmsnorm_kernel(x_ref, g_ref, h_ref):
    h_ref[...] = _rms(x_ref[...], g_ref[...]).astype(h_ref.dtype)


def rmsnorm(x, g, out_dtype, tr=TR):
    m, d = x.shape
    return pl.pallas_call(
        _rmsnorm_kernel,
        grid=(m // tr,),
        in_specs=[pl.BlockSpec((tr, d), lambda i: (i, 0)),
                  pl.BlockSpec((1, d), lambda i: (0, 0))],
        out_specs=pl.BlockSpec((tr, d), lambda i: (i, 0)),
        out_shape=jax.ShapeDtypeStruct((m, d), out_dtype),
        compiler_params=_params(1),
        name="rmsnorm",
    )(x, g.reshape(1, d))


def _add_norm_kernel(x_ref, y_ref, g_ref, xo_ref, h_ref):
    x = x_ref[...] + y_ref[...]
    xo_ref[...] = x
    h_ref[...] = _rms(x, g_ref[...]).astype(h_ref.dtype)


def add_norm(x, y, g, out_dtype, tr=TR):
    m, d = x.shape
    row = pl.BlockSpec((tr, d), lambda i: (i, 0))
    return pl.pallas_call(
        _add_norm_kernel,
        grid=(m // tr,),
        in_specs=[row, row, pl.BlockSpec((1, d), lambda i: (0, 0))],
        out_specs=[row, row],
        out_shape=[jax.ShapeDtypeStruct((m, d), F32), jax.ShapeDtypeStruct((m, d), out_dtype)],
        compiler_params=_params(1),
        name="add_norm",
    )(x, y, g.reshape(1, d))


def _add_norm_router_kernel(x_ref, y_ref, g_ref, wr_ref, xo_ref, h_ref, lg_ref):
    x = x_ref[...] + y_ref[...]
    xo_ref[...] = x
    h = _rms(x, g_ref[...])
    h_ref[...] = h.astype(h_ref.dtype)
    lg_ref[...] = jnp.dot(h.astype(BF16), wr_ref[...].astype(BF16), preferred_element_type=F32)


def add_norm_router(x, y, g, w_router, tr=TR):
    m, d = x.shape
    ne = w_router.shape[1]
    wr = jnp.pad(w_router, ((0, 0), (0, LANES - ne)))
    row = pl.BlockSpec((tr, d), lambda i: (i, 0))
    xo, h, lg = pl.pallas_call(
        _add_norm_router_kernel,
        grid=(m // tr,),
        in_specs=[row, row, pl.BlockSpec((1, d), lambda i: (0, 0)),
                  pl.BlockSpec((d, LANES), lambda i: (0, 0))],
        out_specs=[row, row, pl.BlockSpec((tr, LANES), lambda i: (i, 0))],
        out_shape=[jax.ShapeDtypeStruct((m, d), F32), jax.ShapeDtypeStruct((m, d), F32),
                   jax.ShapeDtypeStruct((m, LANES), F32)],
        compiler_params=_params(1),
        name="add_norm_router",
    )(x, y, g.reshape(1, d), wr)
    return xo, h, lg[:, :ne]


def _prompt_positions():
    return np.tile(np.arange(SEQ, dtype=np.int32), BATCH)


def _sample_positions():
    return np.tile(PAST_LEN + np.arange(DEC_SEQ, dtype=np.int32), DEC_BATCH)


def _permute_rows(a):
    c = a.shape[1:]
    return a.reshape(BATCH, SEQ // PERM, PERM, *c).swapaxes(1, 2).reshape(N_PROMPT, *c)


def _unpermute_rows(a):
    c = a.shape[1:]
    return a.reshape(BATCH, PERM, SEQ // PERM, *c).swapaxes(1, 2).reshape(N_PROMPT, *c)


def rope_tables(head_dim, pos):
    half = head_dim // 2
    pos = jnp.asarray(pos)
    inv = jnp.exp(jnp.arange(half, dtype=F32) * (-2.0 * math.log(ROPE_THETA) / head_dim))
    ang = pos.astype(F32)[:, None] * inv[None, :]
    cos, sin = jnp.cos(ang), jnp.sin(ang)
    reps = LANES // head_dim
    zero = jnp.zeros_like(sin)
    cos_t = jnp.tile(jnp.concatenate([cos, cos], axis=1), (1, reps))
    sin_lo = jnp.tile(jnp.concatenate([-sin, zero], axis=1), (1, reps))
    sin_hi = jnp.tile(jnp.concatenate([zero, sin], axis=1), (1, reps))
    return cos_t, sin_lo, sin_hi


def _rope_chunk(a, cos, sin_lo, sin_hi, half):
    if 2 * half == LANES:
        return a * cos + pltpu.roll(a, half, 1) * (sin_lo + sin_hi)
    return a * cos + pltpu.roll(a, LANES - half, 1) * sin_lo + pltpu.roll(a, half, 1) * sin_hi


def _mm_kernel(*refs, has_bias, rope_cols, rope_half, tn):
    it = iter(refs)
    x_ref, w_ref = next(it), next(it)
    b_ref = next(it) if has_bias else None
    if rope_cols:
        cos_ref, slo_ref, shi_ref = next(it), next(it), next(it)
    o_ref, wb_ref = next(it), next(it)
    j, i = pl.program_id(0), pl.program_id(1)

    @pl.when(i == 0)
    def _():
        wb_ref[...] = w_ref[...].astype(BF16)

    acc = jnp.dot(x_ref[...], wb_ref[...], preferred_element_type=F32)
    if has_bias:
        acc = acc + b_ref[...]
    if not rope_cols:
        o_ref[...] = acc.astype(o_ref.dtype)
        return

    n_chunks = tn // LANES

    def store(n_rope_chunks):
        cos, slo, shi = cos_ref[...], slo_ref[...], shi_ref[...]
        for c in range(n_chunks):
            a = acc[:, c * LANES:(c + 1) * LANES]
            if c < n_rope_chunks:
                a = _rope_chunk(a, cos, slo, shi, rope_half)
            o_ref[:, c * LANES:(c + 1) * LANES] = a.astype(o_ref.dtype)

    full_blocks, rem = divmod(rope_cols, tn)

    @pl.when(j < full_blocks)
    def _():
        store(n_chunks)

    if rem:
        @pl.when(j == full_blocks)
        def _():
            store(rem // LANES)

    @pl.when(j >= full_blocks + (1 if rem else 0))
    def _():
        store(0)


def matmul(x, w, layer, bias=None, rope=None, rope_cols=0, rope_half=0, tm=TM, tn=1024,
           out_dtype=F32):
    m, k = x.shape
    n = w.shape[2]
    assert m % tm == 0 and n % tn == 0 and rope_cols % LANES == 0
    in_specs = [pl.BlockSpec((tm, k), lambda j, i: (i, 0)),
                pl.BlockSpec((None, k, tn), lambda j, i: (layer, 0, j))]
    args = [x, w]
    if bias is not None:
        in_specs.append(pl.BlockSpec((1, tn), lambda j, i: (0, j)))
        args.append(bias.reshape(1, n))
    if rope_cols:
        for t in rope:
            in_specs.append(pl.BlockSpec((tm, LANES), lambda j, i: (i, 0)))
            args.append(t)
    kern = functools.partial(_mm_kernel, has_bias=bias is not None, rope_cols=rope_cols,
                             rope_half=rope_half, tn=tn)
    return pl.pallas_call(
        kern,
        grid=(n // tn, m // tm),
        in_specs=in_specs,
        out_specs=pl.BlockSpec((tm, tn), lambda j, i: (i, j)),
        out_shape=jax.ShapeDtypeStruct((m, n), out_dtype),
        scratch_shapes=[pltpu.VMEM((k, tn), BF16)],
        compiler_params=_params(2),
        name="matmul",
    )(*args)


def _mm_glu_kernel(x_ref, wa_ref, wg_ref, ba_ref, bg_ref, o_ref, wab_ref, wgb_ref):
    @pl.when(pl.program_id(1) == 0)
    def _():
        wab_ref[...] = wa_ref[...].astype(BF16)
        wgb_ref[...] = wg_ref[...].astype(BF16)

    x = x_ref[...]
    a = jnp.dot(x, wab_ref[...], preferred_element_type=F32) + ba_ref[...]
    g = jnp.dot(x, wgb_ref[...], preferred_element_type=F32) + bg_ref[...]
    o_ref[...] = a * jax.nn.sigmoid(g)


def matmul_glu(x, w, layer, bias, tm=TM, tn=512):
    m, k = x.shape
    n = w.shape[2] // 2
    nb = n // tn
    b2 = bias.reshape(1, 2 * n)
    return pl.pallas_call(
        _mm_glu_kernel,
        grid=(nb, m // tm),
        in_specs=[pl.BlockSpec((tm, k), lambda j, i: (i, 0)),
                  pl.BlockSpec((None, k, tn), lambda j, i: (layer, 0, j)),
                  pl.BlockSpec((None, k, tn), lambda j, i: (layer, 0, j + nb)),
                  pl.BlockSpec((1, tn), lambda j, i: (0, j)),
                  pl.BlockSpec((1, tn), lambda j, i: (0, j + nb))],
        out_specs=pl.BlockSpec((tm, tn), lambda j, i: (i, j)),
        out_shape=jax.ShapeDtypeStruct((m, n), F32),
        scratch_shapes=[pltpu.VMEM((k, tn), BF16), pltpu.VMEM((k, tn), BF16)],
        compiler_params=_params(2),
        name="matmul_glu",
    )(x, w, w, b2, b2)


def _swiglu_tile(x, wg, wu, wd):
    g = jnp.dot(x, wg, preferred_element_type=F32)
    u = jnp.dot(x, wu, preferred_element_type=F32)
    h = (g * jax.nn.sigmoid(g) * u).astype(BF16)
    return jnp.dot(h, wd, preferred_element_type=F32)


def _ffn_kernel(x_ref, wg_ref, wu_ref, wd_ref, o_ref):
    @pl.when(pl.program_id(1) == 0)
    def _():
        o_ref[...] = jnp.zeros(o_ref.shape, F32)

    o_ref[...] += _swiglu_tile(x_ref[...], wg_ref[...].astype(BF16), wu_ref[...].astype(BF16),
                               wd_ref[...].astype(BF16))


def swiglu_ffn(h, w_gate, w_up, w_down, layer, tm=FFN_TM, tf=TF):
    m, d = h.shape
    ff = w_gate.shape[2]
    assert m % tm == 0 and ff % tf == 0
    return pl.pallas_call(
        _ffn_kernel,
        grid=(m // tm, ff // tf),
        in_specs=[pl.BlockSpec((tm, d), lambda i, f: (i, 0), pipeline_mode=pl.Buffered(1)),
                  pl.BlockSpec((None, d, tf), lambda i, f: (layer, 0, f)),
                  pl.BlockSpec((None, d, tf), lambda i, f: (layer, 0, f)),
                  pl.BlockSpec((None, tf, d), lambda i, f: (layer, f, 0))],
        out_specs=pl.BlockSpec((tm, d), lambda i, f: (i, 0), pipeline_mode=pl.Buffered(1)),
        out_shape=jax.ShapeDtypeStruct((m, d), F32),
        compiler_params=_params(2),
        name="swiglu_ffn",
    )(h, w_gate, w_up, w_down)


def _moe_kernel(sb_e_ref, sb_nt_ref, x_ref, wg_ref, wu_ref, wd_ref, o_ref,
                wgb_ref, wub_ref, wdb_ref):
    s, f = pl.program_id(0), pl.program_id(1)
    nt = sb_nt_ref[s]

    @pl.when(nt > 0)
    def _():
        wgb_ref[...] = wg_ref[...].astype(BF16)
        wub_ref[...] = wu_ref[...].astype(BF16)
        wdb_ref[...] = wd_ref[...].astype(BF16)

    @pl.when(f == 0)
    def _():
        o_ref[...] = jnp.zeros(o_ref.shape, F32)

    def tile(r, carry):
        rows = pl.ds(pl.multiple_of(r * MOE_TILE, MOE_TILE), MOE_TILE)
        o_ref[rows, :] += _swiglu_tile(x_ref[rows, :], wgb_ref[...], wub_ref[...], wdb_ref[...])
        return carry

    lax.fori_loop(0, nt, tile, 0)


def moe_ffn(x_sorted, sb_expert, sb_ntiles, w_gate, w_up, w_down, layer, tf=TF):
    n_rows, d = x_sorted.shape
    ff = w_gate.shape[3]
    n_sb = n_rows // MOE_SB
    n_f = ff // tf

    def f_idx(s, f, nt_ref):
        return jnp.where(nt_ref[s] > 0, f, n_f - 1)

    grid_spec = pltpu.PrefetchScalarGridSpec(
        num_scalar_prefetch=2,
        grid=(n_sb, n_f),
        in_specs=[pl.BlockSpec((MOE_SB, d), lambda s, f, e, nt: (s, 0),
                               pipeline_mode=pl.Buffered(1)),
                  pl.BlockSpec((None, None, d, tf),
                               lambda s, f, e, nt: (layer, e[s], 0, f_idx(s, f, nt))),
                  pl.BlockSpec((None, None, d, tf),
                               lambda s, f, e, nt: (layer, e[s], 0, f_idx(s, f, nt))),
                  pl.BlockSpec((None, None, tf, d),
                               lambda s, f, e, nt: (layer, e[s], f_idx(s, f, nt), 0))],
        out_specs=pl.BlockSpec((MOE_SB, d), lambda s, f, e, nt: (s, 0),
                               pipeline_mode=pl.Buffered(1)),
        scratch_shapes=[pltpu.VMEM((d, tf), BF16), pltpu.VMEM((d, tf), BF16),
                        pltpu.VMEM((tf, d), BF16)],
    )
    return pl.pallas_call(
        _moe_kernel,
        grid_spec=grid_spec,
        out_shape=jax.ShapeDtypeStruct((n_rows, d), F32),
        compiler_params=_params(2),
        name="moe_ffn",
    )(sb_expert, sb_ntiles, x_sorted, w_gate, w_up, w_down)


def _row_copy(src_ref, src_row, dst_ref, dst_row, sem):
    return pltpu.make_async_copy(src_ref.at[pl.ds(src_row, 1)], dst_ref.at[pl.ds(dst_row, 1)], sem)


def _gather_rows(idx_ref, idx_base, n, src_ref, dst_ref, sem):
    def issue(r, c):
        _row_copy(src_ref, idx_ref[idx_base + r], dst_ref, r, sem).start()
        return c

    def wait(r, c):
        _row_copy(src_ref, 0, dst_ref, r, sem).wait()
        return c

    lax.fori_loop(0, n, issue, 0)
    lax.fori_loop(0, n, wait, 0)


def _dispatch_kernel(tok_ref, live_ref, h_ref, o_ref, buf_ref, sem):
    i = pl.program_id(0)

    @pl.when(live_ref[i] > 0)
    def _():
        _gather_rows(tok_ref, i * MOE_TILE, MOE_TILE, h_ref, buf_ref, sem)
        o_ref[...] = buf_ref[...].astype(o_ref.dtype)

    @pl.when(live_ref[i] == 0)
    def _():
        o_ref[...] = jnp.zeros(o_ref.shape, o_ref.dtype)


def moe_dispatch(h, row_tok, tile_live):
    n_rows = row_tok.shape[0]
    d = h.shape[1]
    grid_spec = pltpu.PrefetchScalarGridSpec(
        num_scalar_prefetch=2,
        grid=(n_rows // MOE_TILE,),
        in_specs=[pl.BlockSpec(memory_space=pl.ANY)],
        out_specs=pl.BlockSpec((MOE_TILE, d), lambda i, tok, live: (i, 0)),
        scratch_shapes=[pltpu.VMEM((MOE_TILE, d), F32), pltpu.SemaphoreType.DMA(())],
    )
    return pl.pallas_call(
        _dispatch_kernel,
        grid_spec=grid_spec,
        out_shape=jax.ShapeDtypeStruct((n_rows, d), BF16),
        compiler_params=_params(1),
        name="moe_dispatch",
    )(row_tok, tile_live, h)


def _combine_kernel(d0_ref, d1_ref, y_ref, x_ref, gate_ref, g_ref, xo_ref, h_ref, a_ref, b_ref, sem,
                    *, tr):
    base = pl.program_id(0) * tr
    _gather_rows(d0_ref, base, tr, y_ref, a_ref, sem)
    _gather_rows(d1_ref, base, tr, y_ref, b_ref, sem)
    gates = gate_ref[...]
    x = x_ref[...] + (a_ref[...] * gates[:, 0:1] + b_ref[...] * gates[:, 1:2])
    xo_ref[...] = x
    h_ref[...] = _rms(x, g_ref[...]).astype(h_ref.dtype)


def moe_combine_add_norm(x, y_sorted, dest, gates, g, out_dtype, tr=TR):
    t, d = x.shape
    gate_pad = jnp.pad(gates, ((0, 0), (0, LANES - TOP_K)))
    row = lambda i, d0, d1: (i, 0)
    grid_spec = pltpu.PrefetchScalarGridSpec(
        num_scalar_prefetch=2,
        grid=(t // tr,),
        in_specs=[pl.BlockSpec(memory_space=pl.ANY),
                  pl.BlockSpec((tr, d), row),
                  pl.BlockSpec((tr, LANES), row),
                  pl.BlockSpec((1, d), lambda i, d0, d1: (0, 0))],
        out_specs=[pl.BlockSpec((tr, d), row), pl.BlockSpec((tr, d), row)],
        scratch_shapes=[pltpu.VMEM((tr, d), F32), pltpu.VMEM((tr, d), F32),
                        pltpu.SemaphoreType.DMA(())],
    )
    return pl.pallas_call(
        functools.partial(_combine_kernel, tr=tr),
        grid_spec=grid_spec,
        out_shape=[jax.ShapeDtypeStruct((t, d), F32), jax.ShapeDtypeStruct((t, d), out_dtype)],
        compiler_params=_params(1),
        name="moe_combine_add_norm",
    )(dest[:, 0], dest[:, 1], y_sorted, x, gate_pad, g.reshape(1, d))


def moe_layer(x, h, logits, w_gate, w_up, w_down, layer, g_next, out_dtype):
    t, d = h.shape
    n_asg = t * TOP_K
    top_logit, top_e = lax.top_k(logits, TOP_K)
    gates = jax.nn.softmax(top_logit, axis=-1)
    flat_e = top_e.reshape(-1).astype(jnp.int32)
    order = jnp.argsort(flat_e).astype(jnp.int32)
    se = flat_e[order]
    counts = jnp.bincount(flat_e, length=N_EXPERTS).astype(jnp.int32)
    start = jnp.cumsum(counts) - counts
    n_sb_e = (counts + MOE_SB - 1) // MOE_SB
    sb_end = jnp.cumsum(n_sb_e)
    sb_start = sb_end - n_sb_e
    per_sb = -(-counts // jnp.maximum(n_sb_e, 1))
    per_sb = jnp.maximum((per_sb + MOE_TILE - 1) // MOE_TILE * MOE_TILE, MOE_TILE)
    pos = jnp.arange(n_asg, dtype=jnp.int32) - start[se]
    dest_sorted = (sb_start[se] + pos // per_sb[se]) * MOE_SB + pos % per_sb[se]
    dest = jnp.zeros((n_asg,), jnp.int32).at[order].set(dest_sorted)
    max_sb = (n_asg + N_EXPERTS * (MOE_SB - 1)) // MOE_SB
    n_rows = max_sb * MOE_SB
    row_tok = jnp.zeros((n_rows,), jnp.int32).at[dest_sorted].set(order // TOP_K)
    total_sb = sb_end[-1]
    sb_ids = jnp.arange(max_sb, dtype=jnp.int32)
    sb_e = jnp.searchsorted(sb_end, jnp.minimum(sb_ids, total_sb - 1), side='right').astype(jnp.int32)
    sb_e = jnp.minimum(sb_e, N_EXPERTS - 1)
    live_rows = jnp.clip(counts[sb_e] - (sb_ids - sb_start[sb_e]) * per_sb[sb_e], 0, per_sb[sb_e])
    sb_nt = jnp.where(sb_ids < total_sb, (live_rows + MOE_TILE - 1) // MOE_TILE, 0).astype(jnp.int32)
    tile_live = (jnp.arange(MOE_SB_TILES, dtype=jnp.int32)[None, :] < sb_nt[:, None])
    tile_live = tile_live.reshape(-1).astype(jnp.int32)

    x_sorted = moe_dispatch(h, row_tok, tile_live)
    y_sorted = moe_ffn(x_sorted, sb_e, sb_nt, w_gate, w_up, w_down, layer)
    return moe_combine_add_norm(x, y_sorted, dest.reshape(t, TOP_K), gates, g_next, out_dtype)


def _band_masks(window, first_block, n_chunks):
    chunk_rows = BLOCK // n_chunks

    def position(axis):
        rr = lax.broadcasted_iota(jnp.int32, (BLOCK, BLOCK), axis)
        return n_chunks * (rr % chunk_rows) + rr // chunk_rows

    d_cur = position(0) - position(1)
    d_prev = d_cur + BLOCK
    m_cur = jnp.logical_and(d_cur >= 0, d_cur <= window)
    m_prev = jnp.logical_and(d_prev <= window, jnp.logical_not(first_block))
    return m_cur, m_prev


def _band_head(q, kp, kc, vp, vc, m_cur, m_prev, scale):
    q = q.astype(BF16)
    s_c = jnp.where(m_cur, _nt_dot(q, kc.astype(BF16)) * scale, NEG)
    s_p = jnp.where(m_prev, _nt_dot(q, kp.astype(BF16)) * scale, NEG)
    m = jnp.maximum(jnp.max(s_c, axis=1, keepdims=True), jnp.max(s_p, axis=1, keepdims=True))
    p_c = jnp.exp(s_c - m)
    p_p = jnp.exp(s_p - m)
    l = jnp.sum(p_c, axis=1, keepdims=True) + jnp.sum(p_p, axis=1, keepdims=True)
    o = (jnp.dot((p_c / l).astype(BF16), vc.astype(BF16), preferred_element_type=F32)
         + jnp.dot((p_p / l).astype(BF16), vp.astype(BF16), preferred_element_type=F32))
    return o, m + jnp.log(l)


def _attn_a_prompt_kernel(q_ref, kp_ref, kc_ref, vp_ref, vc_ref, o_ref, lse_ref, *, window):
    n_chunks, chunk_rows, _ = q_ref.shape
    m_cur, m_prev = _band_masks(window, pl.program_id(2) == 0, n_chunks)
    scale = A_HEAD_DIM ** -0.5
    for h in range(A_HEADS):
        hs = slice(h * A_HEAD_DIM, (h + 1) * A_HEAD_DIM)
        load = lambda ref: ref[:, :, hs].reshape(BLOCK, A_HEAD_DIM)
        o, lse = _band_head(load(q_ref), load(kp_ref), load(kc_ref), load(vp_ref), load(vc_ref),
                            m_cur, m_prev, scale)
        o_ref[:, :, hs] = o.reshape(n_chunks, chunk_rows, A_HEAD_DIM)
        lse_ref[:, :, hs] = jnp.broadcast_to(lse, (BLOCK, A_HEAD_DIM)).reshape(
            n_chunks, chunk_rows, A_HEAD_DIM)


def attn_a_prompt(qkv, g):
    window, dil = A_PATTERNS[g]
    assert PERM % dil == 0 and (SEQ // PERM) % BLOCK == 0
    hw = A_HEADS * A_HEAD_DIM
    n_chunks = PERM // dil
    chunk_rows = BLOCK // n_chunks
    nb = SEQ // dil // BLOCK
    view = qkv.reshape(BATCH, n_chunks, dil, SEQ // PERM, qkv.shape[1])
    blk = (None, n_chunks, None, chunk_rows, hw)

    def cur(which):
        return pl.BlockSpec(blk, lambda b, r, j: (b, 0, r, j, which * A_GROUPS + g))

    def prev(which):
        return pl.BlockSpec(blk, lambda b, r, j: (b, 0, r, jnp.maximum(j - 1, 0),
                                                  which * A_GROUPS + g))

    out_spec = pl.BlockSpec(blk, lambda b, r, j: (b, 0, r, j, 0))
    out_sds = jax.ShapeDtypeStruct((BATCH, n_chunks, dil, SEQ // PERM, hw), F32)
    o, lse = pl.pallas_call(
        functools.partial(_attn_a_prompt_kernel, window=window // dil),
        grid=(BATCH, dil, nb),
        in_specs=[cur(0), prev(1), cur(1), prev(2), cur(2)],
        out_specs=[out_spec, out_spec],
        out_shape=[out_sds, out_sds],
        compiler_params=_params(3),
        name="attn_a_prompt",
    )(view, view, view, view, view)
    return o.reshape(N_PROMPT, hw), lse.reshape(N_PROMPT, hw)


def _attn_b_prompt_kernel(q_ref, kp_ref, kc_ref, vp_ref, vc_ref, sink_ref, o_ref, kbd_ref, vbd_ref):
    rep = B_HEADS // B_KV_HEADS
    n_keys = 2 * BLOCK
    gw = rep * B_HEAD_DIM
    per_col = LANES // B_HEAD_DIM
    scale = B_HEAD_DIM ** -0.5

    @pl.when(jnp.logical_and(pl.program_id(0) == 0, pl.program_id(1) == 0))
    def _():
        kbd_ref[...] = jnp.zeros(kbd_ref.shape, BF16)
        vbd_ref[...] = jnp.zeros(vbd_ref.shape, BF16)

    m_cur, m_prev = _band_masks(B_WINDOW, pl.program_id(1) == 0, 1)
    mask = jnp.concatenate([m_prev, m_cur], axis=1)
    low_half = lax.broadcasted_iota(jnp.int32, (n_keys, LANES), 1) < B_HEAD_DIM

    for g in range(B_KV_HEADS):
        col = slice((g // per_col) * LANES, (g // per_col + 1) * LANES)

        def in_both_halves(p_ref, c_ref):
            x = jnp.concatenate([p_ref[:, col], c_ref[:, col]], axis=0)
            other = pltpu.roll(x, B_HEAD_DIM, 1)
            if g % per_col == 0:
                return jnp.where(low_half, x, other)
            return jnp.where(low_half, other, x)

        kk = in_both_halves(kp_ref, kc_ref)
        vv = in_both_halves(vp_ref, vc_ref)
        for r in range(rep):
            keep = low_half if r % per_col == 0 else jnp.logical_not(low_half)
            rows = slice(r * n_keys, (r + 1) * n_keys)
            cols = slice((r // per_col) * LANES, (r // per_col + 1) * LANES)
            kbd_ref[rows, cols] = jnp.where(keep, kk, 0.0).astype(BF16)
            vbd_ref[rows, cols] = jnp.where(keep, vv, 0.0).astype(BF16)
        q = q_ref[:, g * gw:(g + 1) * gw].astype(BF16)
        s_all = _nt_dot(q, kbd_ref[...]) * scale
        probs = []
        for r in range(rep):
            h = g * rep + r
            s = jnp.where(mask, s_all[:, r * n_keys:(r + 1) * n_keys], NEG)
            sink = sink_ref[:, h:h + 1]
            m = jnp.maximum(jnp.max(s, axis=1, keepdims=True), sink)
            p = jnp.exp(s - m)
            l = jnp.sum(p, axis=1, keepdims=True) + jnp.exp(sink - m)
            probs.append((p / l).astype(BF16))
        o = jnp.dot(jnp.concatenate(probs, axis=1), vbd_ref[...], preferred_element_type=F32)
        o_ref[:, g * gw:(g + 1) * gw] = o.astype(o_ref.dtype)


def attn_b_prompt(qkv, sinks):
    nq = B_HEADS * B_HEAD_DIM
    nk = B_KV_HEADS * B_HEAD_DIM
    rep = B_HEADS // B_KV_HEADS
    nb = SEQ // BLOCK
    kcol, vcol = nq // nk, nq // nk + 1
    assert LANES % B_HEAD_DIM == 0 and B_KV_HEADS % (LANES // B_HEAD_DIM) == 0
    bd_shape = (rep * 2 * BLOCK, rep * B_HEAD_DIM)

    def cur(col, w):
        return pl.BlockSpec((BLOCK, w), lambda b, j: (b * nb + j, col))

    def prev(col, w):
        return pl.BlockSpec((BLOCK, w), lambda b, j: (b * nb + jnp.maximum(j - 1, 0), col))

    sink_row = jnp.pad(sinks.reshape(1, B_HEADS), ((0, 0), (0, LANES - B_HEADS)))
    return pl.pallas_call(
        _attn_b_prompt_kernel,
        grid=(BATCH, nb),
        in_specs=[cur(0, nq), prev(kcol, nk), cur(kcol, nk), prev(vcol, nk), cur(vcol, nk),
                  pl.BlockSpec((1, LANES), lambda b, j: (0, 0))],
        out_specs=pl.BlockSpec((BLOCK, nq), lambda b, j: (b * nb + j, 0)),
        out_shape=jax.ShapeDtypeStruct((N_PROMPT, nq), BF16),
        scratch_shapes=[pltpu.VMEM(bd_shape, BF16), pltpu.VMEM(bd_shape, BF16)],
        compiler_params=_params(2),
        name="attn_b_prompt",
    )(qkv, qkv, qkv, qkv, qkv, sink_row)


def _decode_masks(rows, n_past, window, dil):
    t = lax.broadcasted_iota(jnp.int32, (rows, n_past), 0) % DEC_SEQ
    c = lax.broadcasted_iota(jnp.int32, (rows, n_past), 1)
    gap = n_past + t - c
    m_past = jnp.logical_and((gap & (dil - 1)) == 0, gap <= window)
    t2 = lax.broadcasted_iota(jnp.int32, (rows, DEC_SEQ), 0) % DEC_SEQ
    c2 = lax.broadcasted_iota(jnp.int32, (rows, DEC_SEQ), 1)
    gap2 = t2 - c2
    m_new = jnp.logical_and(jnp.logical_and(gap2 >= 0, (gap2 & (dil - 1)) == 0), gap2 <= window)
    return m_past, m_new


def _decode_softmax(q, k_past, v_past, k_new, v_new, m_past, m_new, scale, sink):
    s_p = jnp.where(m_past, _nt_dot(q, k_past) * scale, NEG)
    s_n = jnp.where(m_new, _nt_dot(q, k_new) * scale, NEG)
    m = jnp.maximum(jnp.max(s_p, axis=1, keepdims=True), jnp.max(s_n, axis=1, keepdims=True))
    if sink is not None:
        m = jnp.maximum(m, sink)
    p_p = jnp.exp(s_p - m)
    p_n = jnp.exp(s_n - m)
    l = jnp.sum(p_p, axis=1, keepdims=True) + jnp.sum(p_n, axis=1, keepdims=True)
    if sink is not None:
        l = l + jnp.exp(sink - m)
    o = (jnp.dot((p_p / l).astype(BF16), v_past, preferred_element_type=F32)
         + jnp.dot((p_n / l).astype(BF16), v_new, preferred_element_type=F32))
    return o, m + jnp.log(l)


def _heads_to_rows(ref):
    return jnp.concatenate([ref[:, h * A_HEAD_DIM:(h + 1) * A_HEAD_DIM] for h in range(A_HEADS)],
                           axis=0)


def _attn_a_decode_kernel(q_ref, kn_ref, vn_ref, c_ref, o_ref, lse_ref, bias_ref, bias_new_ref, *,
                          n_past, window, dil):
    n_sub, res = c_ref.shape[0], c_ref.shape[1]
    n_rows = n_sub * res * A_HEADS
    n_cols = A_HEADS * DEC_SEQ

    @pl.when(pl.program_id(0) == 0)
    def _():
        r = lax.broadcasted_iota(jnp.int32, (n_rows, n_cols), 0)
        c = lax.broadcasted_iota(jnp.int32, (n_rows, n_cols), 1)
        slot = (r // (A_HEADS * res)) * dil + (r // A_HEADS) % res
        gap = n_past + c % DEC_SEQ - slot
        ok = jnp.logical_and(r % A_HEADS == c // DEC_SEQ,
                             jnp.logical_and((gap & (dil - 1)) == 0, gap <= window))
        bias_ref[...] = jnp.where(ok, 0.0, NEG)
        r = lax.broadcasted_iota(jnp.int32, (n_cols, n_cols), 0)
        c = lax.broadcasted_iota(jnp.int32, (n_cols, n_cols), 1)
        gap = c % DEC_SEQ - r % DEC_SEQ
        ok = jnp.logical_and(jnp.logical_and(r // DEC_SEQ == c // DEC_SEQ, gap >= 0),
                             jnp.logical_and((gap & (dil - 1)) == 0, gap <= window))
        bias_new_ref[...] = jnp.where(ok, 0.0, NEG)

    scale = A_HEAD_DIM ** -0.5
    q = _heads_to_rows(q_ref).astype(BF16)
    k_past = c_ref[:, :, 0].reshape(n_rows, A_HEAD_DIM).astype(BF16)
    v_past = c_ref[:, :, 1].reshape(n_rows, A_HEAD_DIM).astype(BF16)
    k_new = _heads_to_rows(kn_ref).astype(BF16)
    v_new = _heads_to_rows(vn_ref).astype(BF16)
    s_p = _nt_dot(k_past, q) * scale + bias_ref[...]
    s_n = _nt_dot(k_new, q) * scale + bias_new_ref[...]
    m = jnp.maximum(jnp.max(s_p, axis=0, keepdims=True), jnp.max(s_n, axis=0, keepdims=True))
    p_p = jnp.exp(s_p - m)
    p_n = jnp.exp(s_n - m)
    l = jnp.sum(p_p, axis=0, keepdims=True) + jnp.sum(p_n, axis=0, keepdims=True)
    tn = (((0,), (0,)), ((), ()))
    o = (lax.dot_general((p_p / l).astype(BF16), v_past, tn, preferred_element_type=F32)
         + lax.dot_general((p_n / l).astype(BF16), v_new, tn, preferred_element_type=F32))
    lse_row = jnp.concatenate([m + jnp.log(l), jnp.zeros((1, LANES - n_cols), F32)], axis=1)
    lse_col = jnp.broadcast_to(lse_row, (LANES, LANES)).T
    lse = lse_col[0:n_cols, :]
    for h in range(A_HEADS):
        hs = slice(h * A_HEAD_DIM, (h + 1) * A_HEAD_DIM)
        o_ref[:, hs] = o[h * DEC_SEQ:(h + 1) * DEC_SEQ]
        lse_ref[:, hs] = lse[h * DEC_SEQ:(h + 1) * DEC_SEQ]


def attn_a_decode(qkv, cache, layer, g):
    window, dil = A_PATTERNS[g]
    hw = A_HEADS * A_HEAD_DIM
    n_past = cache.shape[2]
    assert dil & (dil - 1) == 0 and n_past % dil == 0 and A_HEADS * DEC_SEQ <= LANES
    res = min(dil, DEC_SEQ)
    n_sub = n_past // dil
    cview = cache.reshape(cache.shape[0], DEC_BATCH, n_sub, dil, 2, A_HEADS, A_HEAD_DIM)
    n_rows = n_sub * res * A_HEADS
    n_cols = A_HEADS * DEC_SEQ

    def tok(which):
        return pl.BlockSpec((DEC_SEQ, hw), lambda n: (n, which * A_GROUPS + g))

    out_spec = pl.BlockSpec((DEC_SEQ, hw), lambda n: (n, 0))
    out_sds = jax.ShapeDtypeStruct((N_SAMPLE, hw), F32)
    return pl.pallas_call(
        functools.partial(_attn_a_decode_kernel, n_past=n_past, window=window, dil=dil),
        grid=(DEC_BATCH,),
        in_specs=[tok(0), tok(1), tok(2),
                  pl.BlockSpec((None, None, n_sub, res, 2, A_HEADS, A_HEAD_DIM),
                               lambda n: (layer, n, 0, 0, 0, 0, 0))],
        out_specs=[out_spec, out_spec],
        out_shape=[out_sds, out_sds],
        scratch_shapes=[pltpu.VMEM((n_rows, n_cols), F32), pltpu.VMEM((n_cols, n_cols), F32)],
        compiler_params=_params(1),
        name="attn_a_decode",
    )(qkv, qkv, qkv, cview)


def _merge_groups_kernel(o0, o1, o2, l0, l1, l2, out_ref):
    a, b, c = l0[...], l1[...], l2[...]
    m = jnp.maximum(jnp.maximum(a, b), c)
    wa, wb, wc = jnp.exp(a - m), jnp.exp(b - m), jnp.exp(c - m)
    total = wa + wb + wc
    r = lambda v: v.astype(BF16).astype(F32)
    mix = r(wa / total) * r(o0[...]) + r(wb / total) * r(o1[...]) + r(wc / total) * r(o2[...])
    out_ref[...] = mix.astype(out_ref.dtype)


def merge_groups(outs, lses, tr):
    m, w = outs[0].shape
    row = pl.BlockSpec((tr, w), lambda i: (i, 0))
    return pl.pallas_call(
        _merge_groups_kernel,
        grid=(m // tr,),
        in_specs=[row] * 6,
        out_specs=row,
        out_shape=jax.ShapeDtypeStruct((m, w), BF16),
        compiler_params=_params(1),
        name="merge_groups",
    )(*outs, *lses)


def _attn_b_decode_kernel(q_ref, kn_ref, vn_ref, c_ref, sink_ref, o_ref, *, n_past):
    nk = B_KV_HEADS * B_HEAD_DIM
    rep = B_HEADS // B_KV_HEADS
    rows = rep * DEC_SEQ
    m_past, m_new = _decode_masks(rows, n_past, B_WINDOW, 1)
    pieces = []
    for g in range(B_KV_HEADS):
        gs = slice(g * B_HEAD_DIM, (g + 1) * B_HEAD_DIM)
        q = jnp.concatenate(
            [q_ref[:, (g * rep + r) * B_HEAD_DIM:(g * rep + r + 1) * B_HEAD_DIM] for r in range(rep)],
            axis=0).astype(BF16)
        o, _ = _decode_softmax(q, c_ref[:, gs].astype(BF16),
                               c_ref[:, nk + g * B_HEAD_DIM:nk + (g + 1) * B_HEAD_DIM].astype(BF16),
                               kn_ref[:, gs].astype(BF16), vn_ref[:, gs].astype(BF16),
                               m_past, m_new, B_HEAD_DIM ** -0.5, sink_ref[:, g:g + 1])
        pieces += [o[r * DEC_SEQ:(r + 1) * DEC_SEQ] for r in range(rep)]
    o_ref[...] = jnp.concatenate(pieces, axis=1)


def attn_b_decode(qkv, cache, layer, sinks):
    nq = B_HEADS * B_HEAD_DIM
    nk = B_KV_HEADS * B_HEAD_DIM
    rep = B_HEADS // B_KV_HEADS
    n_past = cache.shape[2]
    cview = cache.reshape(cache.shape[0], DEC_BATCH, n_past, 2 * nk)
    row0 = N_PROMPT // DEC_SEQ
    kcol, vcol = nq // nk, nq // nk + 1
    sink_tab = jnp.repeat(sinks.reshape(B_KV_HEADS, rep).T, DEC_SEQ, axis=0)
    sink_tab = jnp.pad(sink_tab, ((0, 0), (0, LANES - B_KV_HEADS)))
    return pl.pallas_call(
        functools.partial(_attn_b_decode_kernel, n_past=n_past),
        grid=(DEC_BATCH,),
        in_specs=[pl.BlockSpec((DEC_SEQ, nq), lambda n: (row0 + n, 0)),
                  pl.BlockSpec((DEC_SEQ, nk), lambda n: (row0 + n, kcol)),
                  pl.BlockSpec((DEC_SEQ, nk), lambda n: (row0 + n, vcol)),
                  pl.BlockSpec((None, None, n_past, 2 * nk), lambda n: (layer, n, 0, 0)),
                  pl.BlockSpec((rep * DEC_SEQ, LANES), lambda n: (0, 0))],
        out_specs=pl.BlockSpec((DEC_SEQ, nq), lambda n: (n, 0)),
        out_shape=jax.ShapeDtypeStruct((N_SAMPLE, nq), F32),
        compiler_params=_params(1),
        name="attn_b_decode",
    )(qkv, qkv, qkv, cview, sink_tab)


def _conv_kernel(prev_ref, cur_ref, w_ref, b_ref, g_ref, beta_ref, o_ref, ext_ref, conv_ref, *,
                 tl, zero_first):
    d = cur_ref.shape[1]
    prev = prev_ref[...]
    if zero_first:
        prev = jnp.where(pl.program_id(1) == 0, 0.0, prev)
    ext_ref[0:CONV_HALO, :] = prev.astype(BF16).astype(F32)
    ext_ref[CONV_HALO:, :] = cur_ref[...].astype(BF16).astype(F32)
    first = CONV_HALO - (CONV_W - 1)
    for c in range(d // LANES):
        cs = slice(c * LANES, (c + 1) * LANES)
        acc = jnp.zeros((tl, LANES), F32)
        for k in range(CONV_W):
            acc = acc + ext_ref[first + k:first + k + tl, cs] * w_ref[k:k + 1, cs]
        conv_ref[:, cs] = acc
    x = conv_ref[...] + b_ref[...]
    mu = jnp.mean(x, axis=-1, keepdims=True)
    xc = x - mu
    var = jnp.mean(xc * xc, axis=-1, keepdims=True)
    y = xc * lax.rsqrt(var + NORM_EPS) * g_ref[...] + beta_ref[...]
    o_ref[...] = (y * jax.nn.sigmoid(y)).astype(o_ref.dtype)


def conv_module(u, prev_src, prev_block, n_seq, seq_len, row0, w_dw, b_dw, ln_g, ln_b, tl,
                zero_first, out_dtype):
    d = u.shape[1]
    nl = seq_len // tl
    vec = pl.BlockSpec((1, d), lambda b, j: (0, 0))
    kern = functools.partial(_conv_kernel, tl=tl, zero_first=zero_first)
    return pl.pallas_call(
        kern,
        grid=(n_seq, nl),
        in_specs=[pl.BlockSpec((CONV_HALO, d), lambda b, j: (prev_block(b, j), 0)),
                  pl.BlockSpec((tl, d), lambda b, j: (row0 // tl + b * nl + j, 0)),
                  pl.BlockSpec((CONV_W, d), lambda b, j: (0, 0)), vec, vec, vec],
        out_specs=pl.BlockSpec((tl, d), lambda b, j: (b * nl + j, 0)),
        out_shape=jax.ShapeDtypeStruct((n_seq * seq_len, d), out_dtype),
        scratch_shapes=[pltpu.VMEM((CONV_HALO + tl, d), F32), pltpu.VMEM((tl, d), F32)],
        compiler_params=_params(2),
        name="conv_module",
    )(prev_src, u, w_dw, b_dw.reshape(1, d), ln_g.reshape(1, d), ln_b.reshape(1, d))


def _kv_state(k, v, n_seq, seq_len, keep, heads, hd):
    k = k.reshape(n_seq, seq_len, heads, hd)[:, seq_len - keep:]
    v = v.reshape(n_seq, seq_len, heads, hd)[:, seq_len - keep:]
    return jnp.stack([k, v], axis=2)


def mixer_a(h, w_qkv, w_o, caches, layer, rope_p, rope_s):
    hw = A_HEADS * A_HEAD_DIM
    proj = functools.partial(matmul, w=w_qkv, layer=layer, rope_cols=2 * A_GROUPS * hw,
                             rope_half=A_HEAD_DIM // 2, tn=hw)
    qkv_p = proj(_permute_rows(h[:N_PROMPT]), rope=rope_p, tm=TM_PROMPT)
    qkv_s = proj(h[N_PROMPT:], rope=rope_s, tm=N_SAMPLE)
    p_outs, p_lses, s_outs, s_lses, p_states, s_states = [], [], [], [], [], []
    for g, (window, _) in enumerate(A_PATTERNS):
        o, lse = attn_a_prompt(qkv_p, g)
        p_outs.append(o)
        p_lses.append(lse)
        o, lse = attn_a_decode(qkv_s, caches[g], layer, g)
        s_outs.append(o)
        s_lses.append(lse)
        kc = slice((A_GROUPS + g) * hw, (A_GROUPS + g + 1) * hw)
        vc = slice((2 * A_GROUPS + g) * hw, (2 * A_GROUPS + g + 1) * hw)
        p_states.append(_kv_state(_unpermute_rows(qkv_p[:, kc]), _unpermute_rows(qkv_p[:, vc]),
                                  BATCH, SEQ, min(window, SEQ), A_HEADS, A_HEAD_DIM))
        s_states.append(_kv_state(qkv_s[:, kc], qkv_s[:, vc], DEC_BATCH, DEC_SEQ, DEC_SEQ,
                                  A_HEADS, A_HEAD_DIM))
    y_p = matmul(merge_groups(p_outs, p_lses, 512), w_o, layer, tm=TM_PROMPT)
    y_s = matmul(merge_groups(s_outs, s_lses, N_SAMPLE), w_o, layer, tm=N_SAMPLE)
    return jnp.concatenate([_unpermute_rows(y_p), y_s], axis=0), p_states, s_states


def mixer_b(h, w_qkv, b_qkv, sinks, w_o, b_o, cache, layer, rope_b):
    nq = B_HEADS * B_HEAD_DIM
    nk = B_KV_HEADS * B_HEAD_DIM
    qkv = matmul(h, w_qkv, layer, bias=b_qkv, rope=rope_b, rope_cols=nq + nk,
                 rope_half=B_HEAD_DIM // 2, tn=512)
    o_p = attn_b_prompt(qkv, sinks)
    o_s = attn_b_decode(qkv, cache, layer, sinks).astype(BF16)
    k, v = qkv[:, nq:nq + nk], qkv[:, nq + nk:]
    p_state = _kv_state(k[:N_PROMPT], v[:N_PROMPT], BATCH, SEQ, min(B_WINDOW, SEQ),
                        B_KV_HEADS, B_HEAD_DIM)
    s_state = _kv_state(k[N_PROMPT:], v[N_PROMPT:], DEC_BATCH, DEC_SEQ, DEC_SEQ,
                        B_KV_HEADS, B_HEAD_DIM)
    y = matmul(jnp.concatenate([o_p, o_s], axis=0), w_o, layer, bias=b_o)
    return y, p_state, s_state


def mixer_c(h, w_in, b_in, w_dw, b_dw, ln_g, ln_b, w_out, b_out, state, layer):
    keep = CONV_W - 1
    u = matmul_glu(h, w_in, layer, b_in)
    nl = SEQ // CONV_TL
    ratio = CONV_TL // CONV_HALO
    act_p = conv_module(u, u, lambda b, j: jnp.maximum((b * nl + j) * ratio - 1, 0), BATCH, SEQ, 0,
                        w_dw, b_dw, ln_g, ln_b, CONV_TL, True, BF16)
    hist = jnp.pad(state, ((0, 0), (CONV_HALO - keep, 0), (0, 0))).reshape(DEC_BATCH * CONV_HALO, -1)
    act_s = conv_module(u, hist, lambda b, j: b, DEC_BATCH, DEC_SEQ, N_PROMPT,
                        w_dw, b_dw, ln_g, ln_b, DEC_SEQ, False, F32)
    y = matmul(jnp.concatenate([act_p, act_s.astype(BF16)], axis=0), w_out, layer, bias=b_out)
    u_p = u[:N_PROMPT].reshape(BATCH, SEQ, D_MODEL)
    u_s = u[N_PROMPT:].reshape(DEC_BATCH, DEC_SEQ, D_MODEL)
    p_state = u_p[:, SEQ - keep:]
    s_state = jnp.concatenate([state, u_s], axis=1)[:, -keep:]
    return y, p_state, s_state


def kernel(x_prompt, x_sample, cache_a_g0, cache_a_g1, cache_a_g2, cache_b_kv, state_c_conv,
           norm_mix, norm_ffn, norm_final, a_w_qkv, a_w_o, b_w_qkv, b_b_qkv, b_sinks, b_w_o, b_b_o,
           c_w_in, c_b_in, c_w_dw, c_b_dw, c_ln_g, c_ln_b, c_w_out, c_b_out,
           f_w_gate, f_w_up, f_w_down, m_w_router, m_w_gate, m_w_up, m_w_down):
    assert x_prompt.shape == (BATCH, SEQ, D_MODEL) and x_sample.shape == (DEC_BATCH, DEC_SEQ, D_MODEL)
    pos_p, pos_s = _prompt_positions(), _sample_positions()
    rope_a_p = rope_tables(A_HEAD_DIM, np.asarray(_permute_rows(pos_p)))
    rope_a_s = rope_tables(A_HEAD_DIM, pos_s)
    rope_b = rope_tables(B_HEAD_DIM, np.concatenate([pos_p, pos_s]))
    caches_a = (cache_a_g0, cache_a_g1, cache_a_g2)
    x = jnp.concatenate([x_prompt.reshape(N_PROMPT, D_MODEL), x_sample.reshape(N_SAMPLE, D_MODEL)])
    h = rmsnorm(x, norm_mix[0], BF16)
    pa, sa, pb, sb, pc, sc = [], [], [], [], [], []
    for i in range(DEPTH):
        t, j = i % N_MIXERS, i // N_MIXERS
        if t == 0:
            y, p_st, s_st = mixer_a(h, a_w_qkv, a_w_o, caches_a, j, rope_a_p, rope_a_s)
            pa.append(p_st)
            sa.append(s_st)
        elif t == 1:
            y, p_st, s_st = mixer_b(h, b_w_qkv, b_b_qkv[j], b_sinks[j], b_w_o, b_b_o[j],
                                    cache_b_kv, j, rope_b)
            pb.append(p_st)
            sb.append(s_st)
        else:
            y, p_st, s_st = mixer_c(h, c_w_in, c_b_in[j], c_w_dw[j], c_b_dw[j], c_ln_g[j],
                                    c_ln_b[j], c_w_out, c_b_out[j], state_c_conv[j], j)
            pc.append(p_st)
            sc.append(s_st)
        f = i // 2
        g_next, dt_next = (norm_mix[i + 1], BF16) if i + 1 < DEPTH else (norm_final, F32)
        if i % 2 == 0:
            x, h = add_norm(x, y, norm_ffn[i], BF16)
            y = swiglu_ffn(h, f_w_gate, f_w_up, f_w_down, f)
            x, h = add_norm(x, y, g_next, dt_next)
        else:
            x, h, logits = add_norm_router(x, y, norm_ffn[i], m_w_router[f])
            x, h = moe_layer(x, h, logits, m_w_gate, m_w_up, m_w_down, f, g_next, dt_next)
    y_prompt = h[:N_PROMPT].reshape(BATCH, SEQ, D_MODEL)
    y_sample = h[N_PROMPT:].reshape(DEC_BATCH, DEC_SEQ, D_MODEL)
    stack_g = lambda sts, g: jnp.stack([s[g] for s in sts])
    return (y_prompt, y_sample,
            stack_g(pa, 0), stack_g(pa, 1), stack_g(pa, 2), jnp.stack(pb), jnp.stack(pc),
            stack_g(sa, 0), stack_g(sa, 1), stack_g(sa, 2), jnp.stack(sb), jnp.stack(sc))
```

```python
import functools
import math

import jax
import jax.numpy as jnp
import numpy as np
from jax import lax
from jax.experimental import pallas as pl
from jax.experimental.pallas import tpu as pltpu

D_MODEL = 2048
BATCH = 4
SEQ = 2048
DEPTH = 4
DEC_BATCH = 32
DEC_SEQ = 8
PAST_LEN = 16384
N_MIXERS = 3
BLOCK = 128
ROPE_THETA = 10000.0
NORM_EPS = 1e-5
A_PATTERNS = ((128, 1), (512, 4), (2048, 16))
A_GROUPS = len(A_PATTERNS)
A_HEAD_DIM = 128
A_HEADS = D_MODEL // 256
B_HEAD_DIM = 64
B_HEADS = D_MODEL // B_HEAD_DIM
B_KV_HEADS = B_HEADS // 8
B_WINDOW = 128
CONV_W = 31
N_EXPERTS = 8
TOP_K = 2

N_PROMPT = BATCH * SEQ
N_SAMPLE = DEC_BATCH * DEC_SEQ
M_TOK = N_PROMPT + N_SAMPLE

LANES = 128
VMEM_LIMIT_BYTES = 56 * 1024 * 1024

TM = 1056
TM_PROMPT = 1024
FFN_TM = 1056
TR = 528
TF = 256
MOE_TILE = 256
MOE_SB_TILES = 10
MOE_SB = MOE_TILE * MOE_SB_TILES
PERM = max(d for _, d in A_PATTERNS)
CONV_TL = 256
CONV_HALO = 32
NEG = -1e30

BF16 = jnp.bfloat16
F32 = jnp.float32


def _params(n_axes):
    return pltpu.CompilerParams(dimension_semantics=("arbitrary",) * n_axes,
                                vmem_limit_bytes=VMEM_LIMIT_BYTES)


def _nt_dot(a, b):
    return lax.dot_general(a, b, (((1,), (1,)), ((), ())), preferred_element_type=F32)


def _rms(x, g):
    return x * lax.rsqrt(jnp.mean(x * x, axis=-1, keepdims=True) + NORM_EPS) * g


def _rmsnorm_kernel(x_ref, g_ref, h_ref):
    h_ref[...] = _rms(x_ref[...], g_ref[...]).astype(h_ref.dtype)


def rmsnorm(x, g, out_dtype, tr=TR):
    m, d = x.shape
    return pl.pallas_call(
        _rmsnorm_kernel,
        grid=(m // tr,),
        in_specs=[pl.BlockSpec((tr, d), lambda i: (i, 0)),
                  pl.BlockSpec((1, d), lambda i: (0, 0))],
        out_specs=pl.BlockSpec((tr, d), lambda i: (i, 0)),
        out_shape=jax.ShapeDtypeStruct((m, d), out_dtype),
        compiler_params=_params(1),
        name="rmsnorm",
    )(x, g.reshape(1, d))


def _add_norm_kernel(x_ref, y_ref, g_ref, xo_ref, h_ref):
    x = x_ref[...] + y_ref[...]
    xo_ref[...] = x
    h_ref[...] = _rms(x, g_ref[...]).astype(h_ref.dtype)


def add_norm(x, y, g, out_dtype, tr=TR):
    m, d = x.shape
    row = pl.BlockSpec((tr, d), lambda i: (i, 0))
    return pl.pallas_call(
        _add_norm_kernel,
        grid=(m // tr,),
        in_specs=[row, row, pl.BlockSpec((1, d), lambda i: (0, 0))],
        out_specs=[row, row],
        out_shape=[jax.ShapeDtypeStruct((m, d), F32), jax.ShapeDtypeStruct((m, d), out_dtype)],
        compiler_params=_params(1),
        name="add_norm",
    )(x, y, g.reshape(1, d))


def _add_norm_router_kernel(x_ref, y_ref, g_ref, wr_ref, xo_ref, h_ref, lg_ref):
    x = x_ref[...] + y_ref[...]
    xo_ref[...] = x
    h = _rms(x, g_ref[...])
    h_ref[...] = h.astype(h_ref.dtype)
    lg_ref[...] = jnp.dot(h.astype(BF16), wr_ref[...].astype(BF16), preferred_element_type=F32)


def add_norm_router(x, y, g, w_router, tr=TR):
    m, d = x.shape
    ne = w_router.shape[1]
    wr = jnp.pad(w_router, ((0, 0), (0, LANES - ne)))
    row = pl.BlockSpec((tr, d), lambda i: (i, 0))
    xo, h, lg = pl.pallas_call(
        _add_norm_router_kernel,
        grid=(m // tr,),
        in_specs=[row, row, pl.BlockSpec((1, d), lambda i: (0, 0)),
                  pl.BlockSpec((d, LANES), lambda i: (0, 0))],
        out_specs=[row, row, pl.BlockSpec((tr, LANES), lambda i: (i, 0))],
        out_shape=[jax.ShapeDtypeStruct((m, d), F32), jax.ShapeDtypeStruct((m, d), F32),
                   jax.ShapeDtypeStruct((m, LANES), F32)],
        compiler_params=_params(1),
        name="add_norm_router",
    )(x, y, g.reshape(1, d), wr)
    return xo, h, lg[:, :ne]


def _prompt_positions():
    return np.tile(np.arange(SEQ, dtype=np.int32), BATCH)


def _sample_positions():
    return np.tile(PAST_LEN + np.arange(DEC_SEQ, dtype=np.int32), DEC_BATCH)


def _permute_rows(a):
    c = a.shape[1:]
    return a.reshape(BATCH, SEQ // PERM, PERM, *c).swapaxes(1, 2).reshape(N_PROMPT, *c)


def _unpermute_rows(a):
    c = a.shape[1:]
    return a.reshape(BATCH, PERM, SEQ // PERM, *c).swapaxes(1, 2).reshape(N_PROMPT, *c)


def rope_tables(head_dim, pos):
    half = head_dim // 2
    pos = jnp.asarray(pos)
    inv = jnp.exp(jnp.arange(half, dtype=F32) * (-2.0 * math.log(ROPE_THETA) / head_dim))
    ang = pos.astype(F32)[:, None] * inv[None, :]
    cos, sin = jnp.cos(ang), jnp.sin(ang)
    reps = LANES // head_dim
    zero = jnp.zeros_like(sin)
    cos_t = jnp.tile(jnp.concatenate([cos, cos], axis=1), (1, reps))
    sin_lo = jnp.tile(jnp.concatenate([-sin, zero], axis=1), (1, reps))
    sin_hi = jnp.tile(jnp.concatenate([zero, sin], axis=1), (1, reps))
    return cos_t, sin_lo, sin_hi


def _rope_chunk(a, cos, sin_lo, sin_hi, half):
    if 2 * half == LANES:
        return a * cos + pltpu.roll(a, half, 1) * (sin_lo + sin_hi)
    return a * cos + pltpu.roll(a, LANES - half, 1) * sin_lo + pltpu.roll(a, half, 1) * sin_hi


def _mm_kernel(*refs, has_bias, rope_cols, rope_half, tn):
    it = iter(refs)
    x_ref, w_ref = next(it), next(it)
    b_ref = next(it) if has_bias else None
    if rope_cols:
        cos_ref, slo_ref, shi_ref = next(it), next(it), next(it)
    o_ref, wb_ref = next(it), next(it)
    j, i = pl.program_id(0), pl.program_id(1)

    @pl.when(i == 0)
    def _():
        wb_ref[...] = w_ref[...].astype(BF16)

    acc = jnp.dot(x_ref[...], wb_ref[...], preferred_element_type=F32)
    if has_bias:
        acc = acc + b_ref[...]
    if not rope_cols:
        o_ref[...] = acc.astype(o_ref.dtype)
        return

    n_chunks = tn // LANES

    def store(n_rope_chunks):
        cos, slo, shi = cos_ref[...], slo_ref[...], shi_ref[...]
        for c in range(n_chunks):
            a = acc[:, c * LANES:(c + 1) * LANES]
            if c < n_rope_chunks:
                a = _rope_chunk(a, cos, slo, shi, rope_half)
            o_ref[:, c * LANES:(c + 1) * LANES] = a.astype(o_ref.dtype)

    full_blocks, rem = divmod(rope_cols, tn)

    @pl.when(j < full_blocks)
    def _():
        store(n_chunks)

    if rem:
        @pl.when(j == full_blocks)
        def _():
            store(rem // LANES)

    @pl.when(j >= full_blocks + (1 if rem else 0))
    def _():
        store(0)


def matmul(x, w, layer, bias=None, rope=None, rope_cols=0, rope_half=0, tm=TM, tn=1024,
           out_dtype=F32):
    m, k = x.shape
    n = w.shape[2]
    assert m % tm == 0 and n % tn == 0 and rope_cols % LANES == 0
    in_specs = [pl.BlockSpec((tm, k), lambda j, i: (i, 0)),
                pl.BlockSpec((None, k, tn), lambda j, i: (layer, 0, j))]
    args = [x, w]
    if bias is not None:
        in_specs.append(pl.BlockSpec((1, tn), lambda j, i: (0, j)))
        args.append(bias.reshape(1, n))
    if rope_cols:
        for t in rope:
            in_specs.append(pl.BlockSpec((tm, LANES), lambda j, i: (i, 0)))
            args.append(t)
    kern = functools.partial(_mm_kernel, has_bias=bias is not None, rope_cols=rope_cols,
                             rope_half=rope_half, tn=tn)
    return pl.pallas_call(
        kern,
        grid=(n // tn, m // tm),
        in_specs=in_specs,
        out_specs=pl.BlockSpec((tm, tn), lambda j, i: (i, j)),
        out_shape=jax.ShapeDtypeStruct((m, n), out_dtype),
        scratch_shapes=[pltpu.VMEM((k, tn), BF16)],
        compiler_params=_params(2),
        name="matmul",
    )(*args)


def _mm_glu_kernel(x_ref, wa_ref, wg_ref, ba_ref, bg_ref, o_ref, wab_ref, wgb_ref):
    @pl.when(pl.program_id(1) == 0)
    def _():
        wab_ref[...] = wa_ref[...].astype(BF16)
        wgb_ref[...] = wg_ref[...].astype(BF16)

    x = x_ref[...]
    a = jnp.dot(x, wab_ref[...], preferred_element_type=F32) + ba_ref[...]
    g = jnp.dot(x, wgb_ref[...], preferred_element_type=F32) + bg_ref[...]
    o_ref[...] = a * jax.nn.sigmoid(g)


def matmul_glu(x, w, layer, bias, tm=TM, tn=512):
    m, k = x.shape
    n = w.shape[2] // 2
    nb = n // tn
    b2 = bias.reshape(1, 2 * n)
    return pl.pallas_call(
        _mm_glu_kernel,
        grid=(nb, m // tm),
        in_specs=[pl.BlockSpec((tm, k), lambda j, i: (i, 0)),
                  pl.BlockSpec((None, k, tn), lambda j, i: (layer, 0, j)),
                  pl.BlockSpec((None, k, tn), lambda j, i: (layer, 0, j + nb)),
                  pl.BlockSpec((1, tn), lambda j, i: (0, j)),
                  pl.BlockSpec((1, tn), lambda j, i: (0, j + nb))],
        out_specs=pl.BlockSpec((tm, tn), lambda j, i: (i, j)),
        out_shape=jax.ShapeDtypeStruct((m, n), F32),
        scratch_shapes=[pltpu.VMEM((k, tn), BF16), pltpu.VMEM((k, tn), BF16)],
        compiler_params=_params(2),
        name="matmul_glu",
    )(x, w, w, b2, b2)


def _swiglu_tile(x, wg, wu, wd):
    g = jnp.dot(x, wg, preferred_element_type=F32)
    u = jnp.dot(x, wu, preferred_element_type=F32)
    h = (g * jax.nn.sigmoid(g) * u).astype(BF16)
    return jnp.dot(h, wd, preferred_element_type=F32)


def _ffn_kernel(x_ref, wg_ref, wu_ref, wd_ref, o_ref):
    @pl.when(pl.program_id(1) == 0)
    def _():
        o_ref[...] = jnp.zeros(o_ref.shape, F32)

    o_ref[...] += _swiglu_tile(x_ref[...], wg_ref[...].astype(BF16), wu_ref[...].astype(BF16),
                               wd_ref[...].astype(BF16))


def swiglu_ffn(h, w_gate, w_up, w_down, layer, tm=FFN_TM, tf=TF):
    m, d = h.shape
    ff = w_gate.shape[2]
    assert m % tm == 0 and ff % tf == 0
    return pl.pallas_call(
        _ffn_kernel,
        grid=(m // tm, ff // tf),
        in_specs=[pl.BlockSpec((tm, d), lambda i, f: (i, 0), pipeline_mode=pl.Buffered(1)),
                  pl.BlockSpec((None, d, tf), lambda i, f: (layer, 0, f)),
                  pl.BlockSpec((None, d, tf), lambda i, f: (layer, 0, f)),
                  pl.BlockSpec((None, tf, d), lambda i, f: (layer, f, 0))],
        out_specs=pl.BlockSpec((tm, d), lambda i, f: (i, 0), pipeline_mode=pl.Buffered(1)),
        out_shape=jax.ShapeDtypeStruct((m, d), F32),
        compiler_params=_params(2),
        name="swiglu_ffn",
    )(h, w_gate, w_up, w_down)


def _moe_kernel(sb_e_ref, sb_nt_ref, x_ref, wg_ref, wu_ref, wd_ref, o_ref,
                wgb_ref, wub_ref, wdb_ref):
    s, f = pl.program_id(0), pl.program_id(1)
    nt = sb_nt_ref[s]

    @pl.when(nt > 0)
    def _():
        wgb_ref[...] = wg_ref[...].astype(BF16)
        wub_ref[...] = wu_ref[...].astype(BF16)
        wdb_ref[...] = wd_ref[...].astype(BF16)

    @pl.when(f == 0)
    def _():
        o_ref[...] = jnp.zeros(o_ref.shape, F32)

    def run(start, n_rows):
        rows = pl.ds(pl.multiple_of(start, MOE_TILE), n_rows)
        o_ref[rows, :] += _swiglu_tile(x_ref[rows, :], wgb_ref[...], wub_ref[...], wdb_ref[...])

    def pair(r, carry):
        run(r * (2 * MOE_TILE), 2 * MOE_TILE)
        return carry

    lax.fori_loop(0, nt // 2, pair, 0)

    @pl.when(nt % 2 == 1)
    def _():
        run((nt - 1) * MOE_TILE, MOE_TILE)


def moe_ffn(x_sorted, sb_expert, sb_ntiles, w_gate, w_up, w_down, layer, tf=TF):
    n_rows, d = x_sorted.shape
    ff = w_gate.shape[3]
    n_sb = n_rows // MOE_SB
    n_f = ff // tf

    def f_idx(s, f, nt_ref):
        return jnp.where(nt_ref[s] > 0, f, n_f - 1)

    grid_spec = pltpu.PrefetchScalarGridSpec(
        num_scalar_prefetch=2,
        grid=(n_sb, n_f),
        in_specs=[pl.BlockSpec((MOE_SB, d), lambda s, f, e, nt: (s, 0),
                               pipeline_mode=pl.Buffered(1)),
                  pl.BlockSpec((None, None, d, tf),
                               lambda s, f, e, nt: (layer, e[s], 0, f_idx(s, f, nt))),
                  pl.BlockSpec((None, None, d, tf),
                               lambda s, f, e, nt: (layer, e[s], 0, f_idx(s, f, nt))),
                  pl.BlockSpec((None, None, tf, d),
                               lambda s, f, e, nt: (layer, e[s], f_idx(s, f, nt), 0))],
        out_specs=pl.BlockSpec((MOE_SB, d), lambda s, f, e, nt: (s, 0),
                               pipeline_mode=pl.Buffered(1)),
        scratch_shapes=[pltpu.VMEM((d, tf), BF16), pltpu.VMEM((d, tf), BF16),
                        pltpu.VMEM((tf, d), BF16)],
    )
    return pl.pallas_call(
        _moe_kernel,
        grid_spec=grid_spec,
        out_shape=jax.ShapeDtypeStruct((n_rows, d), F32),
        compiler_params=_params(2),
        name="moe_ffn",
    )(sb_expert, sb_ntiles, x_sorted, w_gate, w_up, w_down)


def _row_copy(src_ref, src_row, dst_ref, dst_row, sem):
    return pltpu.make_async_copy(src_ref.at[pl.ds(src_row, 1)], dst_ref.at[pl.ds(dst_row, 1)], sem)


def _gather_rows(idx_ref, idx_base, n, src_ref, dst_ref, sem):
    def issue(r, c):
        _row_copy(src_ref, idx_ref[idx_base + r], dst_ref, r, sem).start()
        return c

    def wait(r, c):
        _row_copy(src_ref, 0, dst_ref, r, sem).wait()
        return c

    lax.fori_loop(0, n, issue, 0)
    lax.fori_loop(0, n, wait, 0)


def _dispatch_kernel(tok_ref, live_ref, h_ref, o_ref, buf_ref, sem):
    i = pl.program_id(0)

    @pl.when(live_ref[i] > 0)
    def _():
        _gather_rows(tok_ref, i * MOE_TILE, MOE_TILE, h_ref, buf_ref, sem)
        o_ref[...] = buf_ref[...].astype(o_ref.dtype)

    @pl.when(live_ref[i] == 0)
    def _():
        o_ref[...] = jnp.zeros(o_ref.shape, o_ref.dtype)


def moe_dispatch(h, row_tok, tile_live):
    n_rows = row_tok.shape[0]
    d = h.shape[1]
    grid_spec = pltpu.PrefetchScalarGridSpec(
        num_scalar_prefetch=2,
        grid=(n_rows // MOE_TILE,),
        in_specs=[pl.BlockSpec(memory_space=pl.ANY)],
        out_specs=pl.BlockSpec((MOE_TILE, d), lambda i, tok, live: (i, 0)),
        scratch_shapes=[pltpu.VMEM((MOE_TILE, d), F32), pltpu.SemaphoreType.DMA(())],
    )
    return pl.pallas_call(
        _dispatch_kernel,
        grid_spec=grid_spec,
        out_shape=jax.ShapeDtypeStruct((n_rows, d), BF16),
        compiler_params=_params(1),
        name="moe_dispatch",
    )(row_tok, tile_live, h)


def _combine_kernel(d0_ref, d1_ref, y_ref, x_ref, gate_ref, g_ref, xo_ref, h_ref, a_ref, b_ref, sem,
                    *, tr):
    base = pl.program_id(0) * tr
    _gather_rows(d0_ref, base, tr, y_ref, a_ref, sem)
    _gather_rows(d1_ref, base, tr, y_ref, b_ref, sem)
    gates = gate_ref[...]
    x = x_ref[...] + (a_ref[...] * gates[:, 0:1] + b_ref[...] * gates[:, 1:2])
    xo_ref[...] = x
    h_ref[...] = _rms(x, g_ref[...]).astype(h_ref.dtype)


def moe_combine_add_norm(x, y_sorted, dest, gates, g, out_dtype, tr=TR):
    t, d = x.shape
    gate_pad = jnp.pad(gates, ((0, 0), (0, LANES - TOP_K)))
    row = lambda i, d0, d1: (i, 0)
    grid_spec = pltpu.PrefetchScalarGridSpec(
        num_scalar_prefetch=2,
        grid=(t // tr,),
        in_specs=[pl.BlockSpec(memory_space=pl.ANY),
                  pl.BlockSpec((tr, d), row),
                  pl.BlockSpec((tr, LANES), row),
                  pl.BlockSpec((1, d), lambda i, d0, d1: (0, 0))],
        out_specs=[pl.BlockSpec((tr, d), row), pl.BlockSpec((tr, d), row)],
        scratch_shapes=[pltpu.VMEM((tr, d), F32), pltpu.VMEM((tr, d), F32),
                        pltpu.SemaphoreType.DMA(())],
    )
    return pl.pallas_call(
        functools.partial(_combine_kernel, tr=tr),
        grid_spec=grid_spec,
        out_shape=[jax.ShapeDtypeStruct((t, d), F32), jax.ShapeDtypeStruct((t, d), out_dtype)],
        compiler_params=_params(1),
        name="moe_combine_add_norm",
    )(dest[:, 0], dest[:, 1], y_sorted, x, gate_pad, g.reshape(1, d))


def moe_layer(x, h, logits, w_gate, w_up, w_down, layer, g_next, out_dtype):
    t, d = h.shape
    n_asg = t * TOP_K
    top_logit, top_e = lax.top_k(logits, TOP_K)
    gates = jax.nn.softmax(top_logit, axis=-1)
    flat_e = top_e.reshape(-1).astype(jnp.int32)
    order = jnp.argsort(flat_e).astype(jnp.int32)
    se = flat_e[order]
    counts = jnp.bincount(flat_e, length=N_EXPERTS).astype(jnp.int32)
    start = jnp.cumsum(counts) - counts
    n_sb_e = (counts + MOE_SB - 1) // MOE_SB
    sb_end = jnp.cumsum(n_sb_e)
    sb_start = sb_end - n_sb_e
    per_sb = -(-counts // jnp.maximum(n_sb_e, 1))
    per_sb = jnp.maximum((per_sb + MOE_TILE - 1) // MOE_TILE * MOE_TILE, MOE_TILE)
    pos = jnp.arange(n_asg, dtype=jnp.int32) - start[se]
    dest_sorted = (sb_start[se] + pos // per_sb[se]) * MOE_SB + pos % per_sb[se]
    dest = jnp.zeros((n_asg,), jnp.int32).at[order].set(dest_sorted)
    max_sb = (n_asg + N_EXPERTS * (MOE_SB - 1)) // MOE_SB
    n_rows = max_sb * MOE_SB
    row_tok = jnp.zeros((n_rows,), jnp.int32).at[dest_sorted].set(order // TOP_K)
    total_sb = sb_end[-1]
    sb_ids = jnp.arange(max_sb, dtype=jnp.int32)
    sb_e = jnp.searchsorted(sb_end, jnp.minimum(sb_ids, total_sb - 1), side='right').astype(jnp.int32)
    sb_e = jnp.minimum(sb_e, N_EXPERTS - 1)
    live_rows = jnp.clip(counts[sb_e] - (sb_ids - sb_start[sb_e]) * per_sb[sb_e], 0, per_sb[sb_e])
    sb_nt = jnp.where(sb_ids < total_sb, (live_rows + MOE_TILE - 1) // MOE_TILE, 0).astype(jnp.int32)
    tile_live = (jnp.arange(MOE_SB_TILES, dtype=jnp.int32)[None, :] < sb_nt[:, None])
    tile_live = tile_live.reshape(-1).astype(jnp.int32)

    x_sorted = moe_dispatch(h, row_tok, tile_live)
    y_sorted = moe_ffn(x_sorted, sb_e, sb_nt, w_gate, w_up, w_down, layer)
    return moe_combine_add_norm(x, y_sorted, dest.reshape(t, TOP_K), gates, g_next, out_dtype)


def _band_masks(window, first_block, n_chunks):
    chunk_rows = BLOCK // n_chunks

    def position(axis):
        rr = lax.broadcasted_iota(jnp.int32, (BLOCK, BLOCK), axis)
        return n_chunks * (rr % chunk_rows) + rr // chunk_rows

    d_cur = position(0) - position(1)
    d_prev = d_cur + BLOCK
    m_cur = jnp.logical_and(d_cur >= 0, d_cur <= window)
    m_prev = jnp.logical_and(d_prev <= window, jnp.logical_not(first_block))
    return m_cur, m_prev


def _attn_a_prompt_kernel(q_ref, kp_ref, kc_ref, vp_ref, vc_ref, o_ref, lse_ref, *, window):
    n_chunks, chunk_rows, _ = q_ref.shape
    m_cur, m_prev = _band_masks(window, pl.program_id(2) == 0, n_chunks)
    scale = A_HEAD_DIM ** -0.5
    heads = [slice(h * A_HEAD_DIM, (h + 1) * A_HEAD_DIM) for h in range(A_HEADS)]
    load = lambda ref, hs: ref[:, :, hs].reshape(BLOCK, A_HEAD_DIM).astype(BF16)

    scores = []
    for hs in heads:
        q = load(q_ref, hs)
        scores.append((jnp.where(m_cur, _nt_dot(q, load(kc_ref, hs)) * scale, NEG),
                       jnp.where(m_prev, _nt_dot(q, load(kp_ref, hs)) * scale, NEG)))
    probs = []
    for hs, (s_c, s_p) in zip(heads, scores):
        m = jnp.maximum(jnp.max(s_c, axis=1, keepdims=True), jnp.max(s_p, axis=1, keepdims=True))
        p_c = jnp.exp(s_c - m)
        p_p = jnp.exp(s_p - m)
        l = jnp.sum(p_c, axis=1, keepdims=True) + jnp.sum(p_p, axis=1, keepdims=True)
        probs.append(((p_c / l).astype(BF16), (p_p / l).astype(BF16)))
        lse_ref[:, :, hs] = jnp.broadcast_to(m + jnp.log(l), (BLOCK, A_HEAD_DIM)).reshape(
            n_chunks, chunk_rows, A_HEAD_DIM)
    for hs, (p_c, p_p) in zip(heads, probs):
        o = (jnp.dot(p_c, load(vc_ref, hs), preferred_element_type=F32)
             + jnp.dot(p_p, load(vp_ref, hs), preferred_element_type=F32))
        o_ref[:, :, hs] = o.reshape(n_chunks, chunk_rows, A_HEAD_DIM)


def attn_a_prompt(qkv, g):
    window, dil = A_PATTERNS[g]
    assert PERM % dil == 0 and (SEQ // PERM) % BLOCK == 0
    hw = A_HEADS * A_HEAD_DIM
    n_chunks = PERM // dil
    chunk_rows = BLOCK // n_chunks
    nb = SEQ // dil // BLOCK
    view = qkv.reshape(BATCH, n_chunks, dil, SEQ // PERM, qkv.shape[1])
    blk = (None, n_chunks, None, chunk_rows, hw)

    def cur(which):
        return pl.BlockSpec(blk, lambda b, r, j: (b, 0, r, j, which * A_GROUPS + g))

    def prev(which):
        return pl.BlockSpec(blk, lambda b, r, j: (b, 0, r, jnp.maximum(j - 1, 0),
                                                  which * A_GROUPS + g))

    out_spec = pl.BlockSpec(blk, lambda b, r, j: (b, 0, r, j, 0))
    out_sds = jax.ShapeDtypeStruct((BATCH, n_chunks, dil, SEQ // PERM, hw), F32)
    o, lse = pl.pallas_call(
        functools.partial(_attn_a_prompt_kernel, window=window // dil),
        grid=(BATCH, dil, nb),
        in_specs=[cur(0), prev(1), cur(1), prev(2), cur(2)],
        out_specs=[out_spec, out_spec],
        out_shape=[out_sds, out_sds],
        compiler_params=_params(3),
        name="attn_a_prompt",
    )(view, view, view, view, view)
    return o.reshape(N_PROMPT, hw), lse.reshape(N_PROMPT, hw)


def _attn_b_prompt_kernel(q_ref, kp_ref, kc_ref, vp_ref, vc_ref, sink_ref, o_ref, kbd_ref, vbd_ref):
    rep = B_HEADS // B_KV_HEADS
    n_keys = 2 * BLOCK
    gw = rep * B_HEAD_DIM
    per_col = LANES // B_HEAD_DIM
    scale = B_HEAD_DIM ** -0.5

    @pl.when(jnp.logical_and(pl.program_id(0) == 0, pl.program_id(1) == 0))
    def _():
        kbd_ref[...] = jnp.zeros(kbd_ref.shape, BF16)
        vbd_ref[...] = jnp.zeros(vbd_ref.shape, BF16)

    m_cur, m_prev = _band_masks(B_WINDOW, pl.program_id(1) == 0, 1)
    mask = jnp.concatenate([m_prev, m_cur], axis=1)
    low_half = lax.broadcasted_iota(jnp.int32, (n_keys, LANES), 1) < B_HEAD_DIM

    for g in range(B_KV_HEADS):
        col = slice((g // per_col) * LANES, (g // per_col + 1) * LANES)

        def in_both_halves(p_ref, c_ref):
            x = jnp.concatenate([p_ref[:, col], c_ref[:, col]], axis=0)
            other = pltpu.roll(x, B_HEAD_DIM, 1)
            if g % per_col == 0:
                return jnp.where(low_half, x, other)
            return jnp.where(low_half, other, x)

        kk = in_both_halves(kp_ref, kc_ref)
        vv = in_both_halves(vp_ref, vc_ref)
        for r in range(rep):
            keep = low_half if r % per_col == 0 else jnp.logical_not(low_half)
            rows = slice(r * n_keys, (r + 1) * n_keys)
            cols = slice((r // per_col) * LANES, (r // per_col + 1) * LANES)
            kbd_ref[rows, cols] = jnp.where(keep, kk, 0.0).astype(BF16)
            vbd_ref[rows, cols] = jnp.where(keep, vv, 0.0).astype(BF16)
        q = q_ref[:, g * gw:(g + 1) * gw].astype(BF16)
        s_all = _nt_dot(q, kbd_ref[...]) * scale
        probs = []
        for r in range(rep):
            h = g * rep + r
            s = jnp.where(mask, s_all[:, r * n_keys:(r + 1) * n_keys], NEG)
            sink = sink_ref[:, h:h + 1]
            m = jnp.maximum(jnp.max(s, axis=1, keepdims=True), sink)
            p = jnp.exp(s - m)
            l = jnp.sum(p, axis=1, keepdims=True) + jnp.exp(sink - m)
            probs.append((p / l).astype(BF16))
        o = jnp.dot(jnp.concatenate(probs, axis=1), vbd_ref[...], preferred_element_type=F32)
        o_ref[:, g * gw:(g + 1) * gw] = o.astype(o_ref.dtype)


def attn_b_prompt(qkv, sinks):
    nq = B_HEADS * B_HEAD_DIM
    nk = B_KV_HEADS * B_HEAD_DIM
    rep = B_HEADS // B_KV_HEADS
    nb = SEQ // BLOCK
    kcol, vcol = nq // nk, nq // nk + 1
    assert LANES % B_HEAD_DIM == 0 and B_KV_HEADS % (LANES // B_HEAD_DIM) == 0
    bd_shape = (rep * 2 * BLOCK, rep * B_HEAD_DIM)

    def cur(col, w):
        return pl.BlockSpec((BLOCK, w), lambda b, j: (b * nb + j, col))

    def prev(col, w):
        return pl.BlockSpec((BLOCK, w), lambda b, j: (b * nb + jnp.maximum(j - 1, 0), col))

    sink_row = jnp.pad(sinks.reshape(1, B_HEADS), ((0, 0), (0, LANES - B_HEADS)))
    return pl.pallas_call(
        _attn_b_prompt_kernel,
        grid=(BATCH, nb),
        in_specs=[cur(0, nq), prev(kcol, nk), cur(kcol, nk), prev(vcol, nk), cur(vcol, nk),
                  pl.BlockSpec((1, LANES), lambda b, j: (0, 0))],
        out_specs=pl.BlockSpec((BLOCK, nq), lambda b, j: (b * nb + j, 0)),
        out_shape=jax.ShapeDtypeStruct((N_PROMPT, nq), BF16),
        scratch_shapes=[pltpu.VMEM(bd_shape, BF16), pltpu.VMEM(bd_shape, BF16)],
        compiler_params=_params(2),
        name="attn_b_prompt",
    )(qkv, qkv, qkv, qkv, qkv, sink_row)


def _decode_masks(rows, n_past, window, dil):
    t = lax.broadcasted_iota(jnp.int32, (rows, n_past), 0) % DEC_SEQ
    c = lax.broadcasted_iota(jnp.int32, (rows, n_past), 1)
    gap = n_past + t - c
    m_past = jnp.logical_and((gap & (dil - 1)) == 0, gap <= window)
    t2 = lax.broadcasted_iota(jnp.int32, (rows, DEC_SEQ), 0) % DEC_SEQ
    c2 = lax.broadcasted_iota(jnp.int32, (rows, DEC_SEQ), 1)
    gap2 = t2 - c2
    m_new = jnp.logical_and(jnp.logical_and(gap2 >= 0, (gap2 & (dil - 1)) == 0), gap2 <= window)
    return m_past, m_new


def _decode_softmax(q, k_past, v_past, k_new, v_new, m_past, m_new, scale, sink):
    s_p = jnp.where(m_past, _nt_dot(q, k_past) * scale, NEG)
    s_n = jnp.where(m_new, _nt_dot(q, k_new) * scale, NEG)
    m = jnp.maximum(jnp.max(s_p, axis=1, keepdims=True), jnp.max(s_n, axis=1, keepdims=True))
    if sink is not None:
        m = jnp.maximum(m, sink)
    p_p = jnp.exp(s_p - m)
    p_n = jnp.exp(s_n - m)
    l = jnp.sum(p_p, axis=1, keepdims=True) + jnp.sum(p_n, axis=1, keepdims=True)
    if sink is not None:
        l = l + jnp.exp(sink - m)
    o = (jnp.dot((p_p / l).astype(BF16), v_past, preferred_element_type=F32)
         + jnp.dot((p_n / l).astype(BF16), v_new, preferred_element_type=F32))
    return o, m + jnp.log(l)


def _heads_to_rows(ref):
    return jnp.concatenate([ref[:, h * A_HEAD_DIM:(h + 1) * A_HEAD_DIM] for h in range(A_HEADS)],
                           axis=0)


def _attn_a_decode_kernel(q_ref, kn_ref, vn_ref, c_ref, o_ref, lse_ref, bias_ref, bias_new_ref, *,
                          n_past, window, dil):
    n_sub, res = c_ref.shape[0], c_ref.shape[1]
    n_rows = n_sub * res * A_HEADS
    n_cols = A_HEADS * DEC_SEQ

    @pl.when(pl.program_id(0) == 0)
    def _():
        r = lax.broadcasted_iota(jnp.int32, (n_rows, n_cols), 0)
        c = lax.broadcasted_iota(jnp.int32, (n_rows, n_cols), 1)
        slot = (r // (A_HEADS * res)) * dil + (r // A_HEADS) % res
        gap = n_past + c % DEC_SEQ - slot
        ok = jnp.logical_and(r % A_HEADS == c // DEC_SEQ,
                             jnp.logical_and((gap & (dil - 1)) == 0, gap <= window))
        bias_ref[...] = jnp.where(ok, 0.0, NEG)
        r = lax.broadcasted_iota(jnp.int32, (n_cols, n_cols), 0)
        c = lax.broadcasted_iota(jnp.int32, (n_cols, n_cols), 1)
        gap = c % DEC_SEQ - r % DEC_SEQ
        ok = jnp.logical_and(jnp.logical_and(r // DEC_SEQ == c // DEC_SEQ, gap >= 0),
                             jnp.logical_and((gap & (dil - 1)) == 0, gap <= window))
        bias_new_ref[...] = jnp.where(ok, 0.0, NEG)

    scale = A_HEAD_DIM ** -0.5
    q = _heads_to_rows(q_ref).astype(BF16)
    k_past = c_ref[:, :, 0].reshape(n_rows, A_HEAD_DIM).astype(BF16)
    v_past = c_ref[:, :, 1].reshape(n_rows, A_HEAD_DIM).astype(BF16)
    k_new = _heads_to_rows(kn_ref).astype(BF16)
    v_new = _heads_to_rows(vn_ref).astype(BF16)
    s_p = _nt_dot(k_past, q) * scale + bias_ref[...]
    s_n = _nt_dot(k_new, q) * scale + bias_new_ref[...]
    m = jnp.maximum(jnp.max(s_p, axis=0, keepdims=True), jnp.max(s_n, axis=0, keepdims=True))
    p_p = jnp.exp(s_p - m)
    p_n = jnp.exp(s_n - m)
    l = jnp.sum(p_p, axis=0, keepdims=True) + jnp.sum(p_n, axis=0, keepdims=True)
    tn = (((0,), (0,)), ((), ()))
    o = (lax.dot_general((p_p / l).astype(BF16), v_past, tn, preferred_element_type=F32)
         + lax.dot_general((p_n / l).astype(BF16), v_new, tn, preferred_element_type=F32))
    lse_row = jnp.concatenate([m + jnp.log(l), jnp.zeros((1, LANES - n_cols), F32)], axis=1)
    lse_col = jnp.broadcast_to(lse_row, (LANES, LANES)).T
    lse = lse_col[0:n_cols, :]
    for h in range(A_HEADS):
        hs = slice(h * A_HEAD_DIM, (h + 1) * A_HEAD_DIM)
        o_ref[:, hs] = o[h * DEC_SEQ:(h + 1) * DEC_SEQ]
        lse_ref[:, hs] = lse[h * DEC_SEQ:(h + 1) * DEC_SEQ]


def attn_a_decode(qkv, cache, layer, g):
    window, dil = A_PATTERNS[g]
    hw = A_HEADS * A_HEAD_DIM
    n_past = cache.shape[2]
    assert dil & (dil - 1) == 0 and n_past % dil == 0 and A_HEADS * DEC_SEQ <= LANES
    res = min(dil, DEC_SEQ)
    n_sub = n_past // dil
    cview = cache.reshape(cache.shape[0], DEC_BATCH, n_sub, dil, 2, A_HEADS, A_HEAD_DIM)
    n_rows = n_sub * res * A_HEADS
    n_cols = A_HEADS * DEC_SEQ

    def tok(which):
        return pl.BlockSpec((DEC_SEQ, hw), lambda n: (n, which * A_GROUPS + g))

    out_spec = pl.BlockSpec((DEC_SEQ, hw), lambda n: (n, 0))
    out_sds = jax.ShapeDtypeStruct((N_SAMPLE, hw), F32)
    return pl.pallas_call(
        functools.partial(_attn_a_decode_kernel, n_past=n_past, window=window, dil=dil),
        grid=(DEC_BATCH,),
        in_specs=[tok(0), tok(1), tok(2),
                  pl.BlockSpec((None, None, n_sub, res, 2, A_HEADS, A_HEAD_DIM),
                               lambda n: (layer, n, 0, 0, 0, 0, 0))],
        out_specs=[out_spec, out_spec],
        out_shape=[out_sds, out_sds],
        scratch_shapes=[pltpu.VMEM((n_rows, n_cols), F32), pltpu.VMEM((n_cols, n_cols), F32)],
        compiler_params=_params(1),
        name="attn_a_decode",
    )(qkv, qkv, qkv, cview)


def _merge_groups_kernel(o0, o1, o2, l0, l1, l2, out_ref):
    a, b, c = l0[...], l1[...], l2[...]
    m = jnp.maximum(jnp.maximum(a, b), c)
    wa, wb, wc = jnp.exp(a - m), jnp.exp(b - m), jnp.exp(c - m)
    total = wa + wb + wc
    r = lambda v: v.astype(BF16).astype(F32)
    mix = r(wa / total) * r(o0[...]) + r(wb / total) * r(o1[...]) + r(wc / total) * r(o2[...])
    out_ref[...] = mix.astype(out_ref.dtype)


def merge_groups(outs, lses, tr):
    m, w = outs[0].shape
    row = pl.BlockSpec((tr, w), lambda i: (i, 0))
    return pl.pallas_call(
        _merge_groups_kernel,
        grid=(m // tr,),
        in_specs=[row] * 6,
        out_specs=row,
        out_shape=jax.ShapeDtypeStruct((m, w), BF16),
        compiler_params=_params(1),
        name="merge_groups",
    )(*outs, *lses)


def _attn_b_decode_kernel(q_ref, kn_ref, vn_ref, c_ref, sink_ref, o_ref, *, n_past):
    nk = B_KV_HEADS * B_HEAD_DIM
    rep = B_HEADS // B_KV_HEADS
    rows = rep * DEC_SEQ
    m_past, m_new = _decode_masks(rows, n_past, B_WINDOW, 1)
    pieces = []
    for g in range(B_KV_HEADS):
        gs = slice(g * B_HEAD_DIM, (g + 1) * B_HEAD_DIM)
        q = jnp.concatenate(
            [q_ref[:, (g * rep + r) * B_HEAD_DIM:(g * rep + r + 1) * B_HEAD_DIM] for r in range(rep)],
            axis=0).astype(BF16)
        o, _ = _decode_softmax(q, c_ref[:, gs].astype(BF16),
                               c_ref[:, nk + g * B_HEAD_DIM:nk + (g + 1) * B_HEAD_DIM].astype(BF16),
                               kn_ref[:, gs].astype(BF16), vn_ref[:, gs].astype(BF16),
                               m_past, m_new, B_HEAD_DIM ** -0.5, sink_ref[:, g:g + 1])
        pieces += [o[r * DEC_SEQ:(r + 1) * DEC_SEQ] for r in range(rep)]
    o_ref[...] = jnp.concatenate(pieces, axis=1)


def attn_b_decode(qkv, cache, layer, sinks):
    nq = B_HEADS * B_HEAD_DIM
    nk = B_KV_HEADS * B_HEAD_DIM
    rep = B_HEADS // B_KV_HEADS
    n_past = cache.shape[2]
    cview = cache.reshape(cache.shape[0], DEC_BATCH, n_past, 2 * nk)
    row0 = N_PROMPT // DEC_SEQ
    kcol, vcol = nq // nk, nq // nk + 1
    sink_tab = jnp.repeat(sinks.reshape(B_KV_HEADS, rep).T, DEC_SEQ, axis=0)
    sink_tab = jnp.pad(sink_tab, ((0, 0), (0, LANES - B_KV_HEADS)))
    return pl.pallas_call(
        functools.partial(_attn_b_decode_kernel, n_past=n_past),
        grid=(DEC_BATCH,),
        in_specs=[pl.BlockSpec((DEC_SEQ, nq), lambda n: (row0 + n, 0)),
                  pl.BlockSpec((DEC_SEQ, nk), lambda n: (row0 + n, kcol)),
                  pl.BlockSpec((DEC_SEQ, nk), lambda n: (row0 + n, vcol)),
                  pl.BlockSpec((None, None, n_past, 2 * nk), lambda n: (layer, n, 0, 0)),
                  pl.BlockSpec((rep * DEC_SEQ, LANES), lambda n: (0, 0))],
        out_specs=pl.BlockSpec((DEC_SEQ, nq), lambda n: (n, 0)),
        out_shape=jax.ShapeDtypeStruct((N_SAMPLE, nq), F32),
        compiler_params=_params(1),
        name="attn_b_decode",
    )(qkv, qkv, qkv, cview, sink_tab)


def _conv_kernel(prev_ref, cur_ref, w_ref, b_ref, g_ref, beta_ref, o_ref, ext_ref, conv_ref, *,
                 tl, zero_first):
    d = cur_ref.shape[1]
    prev = prev_ref[...]
    if zero_first:
        prev = jnp.where(pl.program_id(1) == 0, 0.0, prev)
    ext_ref[0:CONV_HALO, :] = prev.astype(BF16).astype(F32)
    ext_ref[CONV_HALO:, :] = cur_ref[...].astype(BF16).astype(F32)
    first = CONV_HALO - (CONV_W - 1)
    for c in range(d // LANES):
        cs = slice(c * LANES, (c + 1) * LANES)
        acc = jnp.zeros((tl, LANES), F32)
        for k in range(CONV_W):
            acc = acc + ext_ref[first + k:first + k + tl, cs] * w_ref[k:k + 1, cs]
        conv_ref[:, cs] = acc
    x = conv_ref[...] + b_ref[...]
    mu = jnp.mean(x, axis=-1, keepdims=True)
    xc = x - mu
    var = jnp.mean(xc * xc, axis=-1, keepdims=True)
    y = xc * lax.rsqrt(var + NORM_EPS) * g_ref[...] + beta_ref[...]
    o_ref[...] = (y * jax.nn.sigmoid(y)).astype(o_ref.dtype)


def conv_module(u, prev_src, prev_block, n_seq, seq_len, row0, w_dw, b_dw, ln_g, ln_b, tl,
                zero_first, out_dtype):
    d = u.shape[1]
    nl = seq_len // tl
    vec = pl.BlockSpec((1, d), lambda b, j: (0, 0))
    kern = functools.partial(_conv_kernel, tl=tl, zero_first=zero_first)
    return pl.pallas_call(
        kern,
        grid=(n_seq, nl),
        in_specs=[pl.BlockSpec((CONV_HALO, d), lambda b, j: (prev_block(b, j), 0)),
                  pl.BlockSpec((tl, d), lambda b, j: (row0 // tl + b * nl + j, 0)),
                  pl.BlockSpec((CONV_W, d), lambda b, j: (0, 0)), vec, vec, vec],
        out_specs=pl.BlockSpec((tl, d), lambda b, j: (b * nl + j, 0)),
        out_shape=jax.ShapeDtypeStruct((n_seq * seq_len, d), out_dtype),
        scratch_shapes=[pltpu.VMEM((CONV_HALO + tl, d), F32), pltpu.VMEM((tl, d), F32)],
        compiler_params=_params(2),
        name="conv_module",
    )(prev_src, u, w_dw, b_dw.reshape(1, d), ln_g.reshape(1, d), ln_b.reshape(1, d))


def _kv_state(k, v, n_seq, seq_len, keep, heads, hd):
    k = k.reshape(n_seq, seq_len, heads, hd)[:, seq_len - keep:]
    v = v.reshape(n_seq, seq_len, heads, hd)[:, seq_len - keep:]
    return jnp.stack([k, v], axis=2)


def mixer_a(h, w_qkv, w_o, caches, layer, rope_p, rope_s):
    hw = A_HEADS * A_HEAD_DIM
    proj = functools.partial(matmul, w=w_qkv, layer=layer, rope_cols=2 * A_GROUPS * hw,
                             rope_half=A_HEAD_DIM // 2, tn=hw)
    qkv_p = proj(_permute_rows(h[:N_PROMPT]), rope=rope_p, tm=TM_PROMPT)
    qkv_s = proj(h[N_PROMPT:], rope=rope_s, tm=N_SAMPLE)
    p_outs, p_lses, s_outs, s_lses, p_states, s_states = [], [], [], [], [], []
    for g, (window, _) in enumerate(A_PATTERNS):
        o, lse = attn_a_prompt(qkv_p, g)
        p_outs.append(o)
        p_lses.append(lse)
        o, lse = attn_a_decode(qkv_s, caches[g], layer, g)
        s_outs.append(o)
        s_lses.append(lse)
        kc = slice((A_GROUPS + g) * hw, (A_GROUPS + g + 1) * hw)
        vc = slice((2 * A_GROUPS + g) * hw, (2 * A_GROUPS + g + 1) * hw)
        p_states.append(_kv_state(_unpermute_rows(qkv_p[:, kc]), _unpermute_rows(qkv_p[:, vc]),
                                  BATCH, SEQ, min(window, SEQ), A_HEADS, A_HEAD_DIM))
        s_states.append(_kv_state(qkv_s[:, kc], qkv_s[:, vc], DEC_BATCH, DEC_SEQ, DEC_SEQ,
                                  A_HEADS, A_HEAD_DIM))
    y_p = matmul(merge_groups(p_outs, p_lses, 512), w_o, layer, tm=TM_PROMPT)
    y_s = matmul(merge_groups(s_outs, s_lses, N_SAMPLE), w_o, layer, tm=N_SAMPLE)
    return jnp.concatenate([_unpermute_rows(y_p), y_s], axis=0), p_states, s_states


def mixer_b(h, w_qkv, b_qkv, sinks, w_o, b_o, cache, layer, rope_b):
    nq = B_HEADS * B_HEAD_DIM
    nk = B_KV_HEADS * B_HEAD_DIM
    qkv = matmul(h, w_qkv, layer, bias=b_qkv, rope=rope_b, rope_cols=nq + nk,
                 rope_half=B_HEAD_DIM // 2, tn=512)
    o_p = attn_b_prompt(qkv, sinks)
    o_s = attn_b_decode(qkv, cache, layer, sinks).astype(BF16)
    k, v = qkv[:, nq:nq + nk], qkv[:, nq + nk:]
    p_state = _kv_state(k[:N_PROMPT], v[:N_PROMPT], BATCH, SEQ, min(B_WINDOW, SEQ),
                        B_KV_HEADS, B_HEAD_DIM)
    s_state = _kv_state(k[N_PROMPT:], v[N_PROMPT:], DEC_BATCH, DEC_SEQ, DEC_SEQ,
                        B_KV_HEADS, B_HEAD_DIM)
    y = matmul(jnp.concatenate([o_p, o_s], axis=0), w_o, layer, bias=b_o)
    return y, p_state, s_state


def mixer_c(h, w_in, b_in, w_dw, b_dw, ln_g, ln_b, w_out, b_out, state, layer):
    keep = CONV_W - 1
    u = matmul_glu(h, w_in, layer, b_in)
    nl = SEQ // CONV_TL
    ratio = CONV_TL // CONV_HALO
    act_p = conv_module(u, u, lambda b, j: jnp.maximum((b * nl + j) * ratio - 1, 0), BATCH, SEQ, 0,
                        w_dw, b_dw, ln_g, ln_b, CONV_TL, True, BF16)
    hist = jnp.pad(state, ((0, 0), (CONV_HALO - keep, 0), (0, 0))).reshape(DEC_BATCH * CONV_HALO, -1)
    act_s = conv_module(u, hist, lambda b, j: b, DEC_BATCH, DEC_SEQ, N_PROMPT,
                        w_dw, b_dw, ln_g, ln_b, DEC_SEQ, False, F32)
    y = matmul(jnp.concatenate([act_p, act_s.astype(BF16)], axis=0), w_out, layer, bias=b_out)
    u_p = u[:N_PROMPT].reshape(BATCH, SEQ, D_MODEL)
    u_s = u[N_PROMPT:].reshape(DEC_BATCH, DEC_SEQ, D_MODEL)
    p_state = u_p[:, SEQ - keep:]
    s_state = jnp.concatenate([state, u_s], axis=1)[:, -keep:]
    return y, p_state, s_state


def kernel(x_prompt, x_sample, cache_a_g0, cache_a_g1, cache_a_g2, cache_b_kv, state_c_conv,
           norm_mix, norm_ffn, norm_final, a_w_qkv, a_w_o, b_w_qkv, b_b_qkv, b_sinks, b_w_o, b_b_o,
           c_w_in, c_b_in, c_w_dw, c_b_dw, c_ln_g, c_ln_b, c_w_out, c_b_out,
           f_w_gate, f_w_up, f_w_down, m_w_router, m_w_gate, m_w_up, m_w_down):
    assert x_prompt.shape == (BATCH, SEQ, D_MODEL) and x_sample.shape == (DEC_BATCH, DEC_SEQ, D_MODEL)
    pos_p, pos_s = _prompt_positions(), _sample_positions()
    rope_a_p = rope_tables(A_HEAD_DIM, np.asarray(_permute_rows(pos_p)))
    rope_a_s = rope_tables(A_HEAD_DIM, pos_s)
    rope_b = rope_tables(B_HEAD_DIM, np.concatenate([pos_p, pos_s]))
    caches_a = (cache_a_g0, cache_a_g1, cache_a_g2)
    x = jnp.concatenate([x_prompt.reshape(N_PROMPT, D_MODEL), x_sample.reshape(N_SAMPLE, D_MODEL)])
    h = rmsnorm(x, norm_mix[0], BF16)
    pa, sa, pb, sb, pc, sc = [], [], [], [], [], []
    for i in range(DEPTH):
        t, j = i % N_MIXERS, i // N_MIXERS
        if t == 0:
            y, p_st, s_st = mixer_a(h, a_w_qkv, a_w_o, caches_a, j, rope_a_p, rope_a_s)
            pa.append(p_st)
            sa.append(s_st)
        elif t == 1:
            y, p_st, s_st = mixer_b(h, b_w_qkv, b_b_qkv[j], b_sinks[j], b_w_o, b_b_o[j],
                                    cache_b_kv, j, rope_b)
            pb.append(p_st)
            sb.append(s_st)
        else:
            y, p_st, s_st = mixer_c(h, c_w_in, c_b_in[j], c_w_dw[j], c_b_dw[j], c_ln_g[j],
                                    c_ln_b[j], c_w_out, c_b_out[j], state_c_conv[j], j)
            pc.append(p_st)
            sc.append(s_st)
        f = i // 2
        g_next, dt_next = (norm_mix[i + 1], BF16) if i + 1 < DEPTH else (norm_final, F32)
        if i % 2 == 0:
            x, h = add_norm(x, y, norm_ffn[i], BF16)
            y = swiglu_ffn(h, f_w_gate, f_w_up, f_w_down, f)
            x, h = add_norm(x, y, g_next, dt_next)
        else:
            x, h, logits = add_norm_router(x, y, norm_ffn[i], m_w_router[f])
            x, h = moe_layer(x, h, logits, m_w_gate, m_w_up, m_w_down, f, g_next, dt_next)
    y_prompt = h[:N_PROMPT].reshape(BATCH, SEQ, D_MODEL)
    y_sample = h[N_PROMPT:].reshape(DEC_BATCH, DEC_SEQ, D_MODEL)
    stack_g = lambda sts, g: jnp.stack([s[g] for s in sts])
    return (y_prompt, y_sample,
            stack_g(pa, 0), stack_g(pa, 1), stack_g(pa, 2), jnp.stack(pb), jnp.stack(pc),
            stack_g(sa, 0), stack_g(sa, 1), stack_g(sa, 2), jnp.stack(sb), jnp.stack(sc))
```
